```python
import jax, jax.numpy as jnp
from jax import lax
import numpy as np

D_MODEL = 1024
BATCH = 2
SEQ = 8192
DEPTH = 2

CHUNK = 64
CONV_W = 4
LRU_WIDTH = D_MODEL
LRU_BLOCKS = 16
LRU_BLOCK_DIM = LRU_WIDTH // LRU_BLOCKS
LRU_C = 8.0
SSD_EXPAND = 2
SSD_INNER = SSD_EXPAND * D_MODEL
SSD_HEAD_DIM = 64
SSD_HEADS = SSD_INNER // SSD_HEAD_DIM
SSD_GROUPS = 4
SSD_HEADS_PER_GROUP = SSD_HEADS // SSD_GROUPS
SSD_STATE = 128
SSD_CONV_DIM = SSD_INNER + 2 * SSD_GROUPS * SSD_STATE
N_BRANCH = 2
D_FF = ((8 * D_MODEL + 3 * 256 - 1) // (3 * 256)) * 256
EPS = 1e-6
IN_WIDTHS = (LRU_WIDTH, LRU_WIDTH, SSD_INNER, SSD_CONV_DIM, SSD_HEADS, N_BRANCH * D_MODEL)
IN_DIM = sum(IN_WIDTHS)

kernel_name = "hybrid_rglru_ssd_parallel_gated_block"


def _split(t, widths):
    offs = np.cumsum(widths)[:-1].tolist()
    return jnp.split(t, offs, axis=-1)


def rms_norm(x, g):
    xf = x.astype(jnp.float32)
    y = xf * lax.rsqrt(jnp.mean(xf * xf, axis=-1, keepdims=True) + EPS)
    return (y * g.astype(jnp.float32)).astype(x.dtype)


def causal_dw_conv(x, w, b):
    y = lax.conv_general_dilated(
        x, w[:, None, :].astype(x.dtype), window_strides=(1,), padding=[(CONV_W - 1, 0)],
        dimension_numbers=('NWC', 'WIO', 'NWC'), feature_group_count=x.shape[-1])
    return y + b.astype(x.dtype)


def rg_lru(x, w_a, b_a, w_x, b_x, lam):
    bsz, s, w = x.shape
    f32 = jnp.float32
    xf = x.astype(f32)
    xb = xf.reshape(bsz, s, LRU_BLOCKS, LRU_BLOCK_DIM)
    r = jax.nn.sigmoid(jnp.einsum('bshi,hij->bshj', xb, w_a.astype(f32)).reshape(bsz, s, w) + b_a.astype(f32))
    i = jax.nn.sigmoid(jnp.einsum('bshi,hij->bshj', xb, w_x.astype(f32)).reshape(bsz, s, w) + b_x.astype(f32))
    log_a = -LRU_C * r * jax.nn.softplus(-lam.astype(f32))
    a = jnp.exp(log_a)
    u = jnp.sqrt(-jnp.expm1(2.0 * log_a)) * (i * xf)

    def combine(lhs, rhs):
        a1, b1 = lhs
        a2, b2 = rhs
        return a1 * a2, a2 * b1 + b2

    _, h = lax.associative_scan(combine, (a, u), axis=1)
    return h.astype(x.dtype)


def ssd_scan(x, dt, A, Bm, Cm):
    b, s = x.shape[:2]
    c = s // CHUNK
    G, K, P, N = SSD_GROUPS, SSD_HEADS_PER_GROUP, SSD_HEAD_DIM, SSD_STATE
    xdt = (x * dt[..., None]).reshape(b, c, CHUNK, G, K, P)
    a = (dt * A).reshape(b, c, CHUNK, G, K)
    Bc = Bm.reshape(b, c, CHUNK, G, N)
    Cc = Cm.reshape(b, c, CHUNK, G, N)
    a_cs = jnp.cumsum(a, axis=2)
    seg = a_cs[:, :, :, None] - a_cs[:, :, None, :]
    causal = jnp.tril(jnp.ones((CHUNK, CHUNK), dtype=bool))[None, None, :, :, None, None]
    decay = jnp.exp(jnp.where(causal, seg, -jnp.inf))
    scores = jnp.einsum('bclgn,bcsgn->bclsg', Cc, Bc)
    y_diag = jnp.einsum('bclsg,bclsgk,bcsgkp->bclgkp', scores, decay, xdt)
    decay_to_end = jnp.exp(a_cs[:, :, -1:] - a_cs)
    states = jnp.einsum('bclgn,bclgk,bclgkp->bcgkpn', Bc, decay_to_end, xdt)
    chunk_decay = jnp.exp(a_cs[:, :, -1])

    def step(h, inp):
        st, dc = inp
        return h * dc[..., None, None] + st, h

    h0 = jnp.zeros((b, G, K, P, N), x.dtype)
    _, prev = lax.scan(step, h0, (jnp.moveaxis(states, 1, 0), jnp.moveaxis(chunk_decay, 1, 0)))
    prev = jnp.moveaxis(prev, 0, 1)
    y_off = jnp.einsum('bclgn,bcgkpn,bclgk->bclgkp', Cc, prev, jnp.exp(a_cs))
    return (y_diag + y_off).reshape(b, s, G * K, P)


def hybrid_mixer(xn, w_in, b_gate, lru_conv_w, lru_conv_b, lru_w_a, lru_b_a, lru_w_x, lru_b_x,
                 lru_lambda, ssd_conv_w, ssd_conv_b, ssd_dt_bias, ssd_A_log, ssd_D, ssd_norm_g,
                 w_branch, w_out):
    bsz, s, _ = xn.shape
    f32 = jnp.float32
    proj = xn @ w_in
    lru_x, lru_gate, z, xbc, dt_raw, gates = _split(proj, IN_WIDTHS)
    u = causal_dw_conv(lru_x, lru_conv_w, lru_conv_b)
    h = rg_lru(u, lru_w_a, lru_b_a, lru_w_x, lru_b_x, lru_lambda)
    y_a = jax.nn.gelu(lru_gate) * h
    xbc = jax.nn.silu(causal_dw_conv(xbc, ssd_conv_w, ssd_conv_b))
    xs, Bm, Cm = _split(xbc, (SSD_INNER, SSD_GROUPS * SSD_STATE, SSD_GROUPS * SSD_STATE))
    dt = jax.nn.softplus(dt_raw.astype(f32) + ssd_dt_bias.astype(f32))
    A = -jnp.exp(ssd_A_log.astype(f32))
    xh = xs.astype(f32).reshape(bsz, s, SSD_HEADS, SSD_HEAD_DIM)
    y = ssd_scan(xh, dt, A,
                 Bm.astype(f32).reshape(bsz, s, SSD_GROUPS, SSD_STATE),
                 Cm.astype(f32).reshape(bsz, s, SSD_GROUPS, SSD_STATE))
    y = y + ssd_D.astype(f32)[:, None] * xh
    y = y.reshape(bsz, s, SSD_INNER) * jax.nn.silu(z.astype(f32))
    yg = y.reshape(bsz, s, SSD_GROUPS, SSD_INNER // SSD_GROUPS)
    yg = yg * lax.rsqrt(jnp.mean(yg * yg, axis=-1, keepdims=True) + EPS)
    y_b = (yg.reshape(bsz, s, SSD_INNER) * ssd_norm_g.astype(f32)).astype(xn.dtype)
    g = jax.nn.sigmoid(gates + b_gate)
    g_a, g_b = _split(g, (D_MODEL, D_MODEL))
    merged = g_a * (y_a @ w_branch[:LRU_WIDTH]) + g_b * (y_b @ w_branch[LRU_WIDTH:])
    return merged @ w_out


def swiglu(xn, w_ffn_in, w_ffn_out):
    gate, up = _split(xn @ w_ffn_in, (D_FF, D_FF))
    return (jax.nn.silu(gate) * up) @ w_ffn_out


def setup_inputs(seed: int = 0) -> dict:
    key = jax.random.key(seed)
    ks = jax.random.split(key, 24)
    nrm = lambda k, shape, scale: jax.random.normal(k, shape, jnp.float32) * scale
    L = DEPTH
    a_c = jax.random.uniform(ks[9], (L, LRU_WIDTH), jnp.float32, 0.9, 0.999)
    sig = a_c ** (1.0 / LRU_C)
    lru_lambda = jnp.log(sig) - jnp.log1p(-sig)
    dt0 = jnp.exp(jax.random.uniform(ks[12], (L, SSD_HEADS), jnp.float32, np.log(1e-3), np.log(1e-1)))
    ssd_dt_bias = dt0 + jnp.log(-jnp.expm1(-dt0))
    ssd_A_log = jnp.log(jax.random.uniform(ks[13], (L, SSD_HEADS), jnp.float32, 1.0, 16.0))
    w_branch = jnp.concatenate([
        nrm(ks[16], (L, LRU_WIDTH, D_MODEL), LRU_WIDTH ** -0.5),
        nrm(ks[17], (L, SSD_INNER, D_MODEL), SSD_INNER ** -0.5)], axis=1)
    return {
        "x": nrm(ks[0], (BATCH, SEQ, D_MODEL), 1.0),
        "norm1_g": 1.0 + nrm(ks[1], (L, D_MODEL), 0.02),
        "w_in": nrm(ks[2], (L, D_MODEL, IN_DIM), D_MODEL ** -0.5),
        "b_gate": nrm(ks[3], (L, N_BRANCH * D_MODEL), 0.02),
        "lru_conv_w": nrm(ks[4], (L, CONV_W, LRU_WIDTH), CONV_W ** -0.5),
        "lru_conv_b": nrm(ks[5], (L, LRU_WIDTH), 0.02),
        "lru_w_a": nrm(ks[6], (L, LRU_BLOCKS, LRU_BLOCK_DIM, LRU_BLOCK_DIM), LRU_BLOCK_DIM ** -0.5),
        "lru_b_a": nrm(ks[7], (L, LRU_WIDTH), 0.02),
        "lru_w_x": nrm(ks[8], (L, LRU_BLOCKS, LRU_BLOCK_DIM, LRU_BLOCK_DIM), LRU_BLOCK_DIM ** -0.5),
        "lru_b_x": nrm(ks[18], (L, LRU_WIDTH), 0.02),
        "lru_lambda": lru_lambda,
        "ssd_conv_w": nrm(ks[10], (L, CONV_W, SSD_CONV_DIM), CONV_W ** -0.5),
        "ssd_conv_b": nrm(ks[11], (L, SSD_CONV_DIM), 0.02),
        "ssd_dt_bias": ssd_dt_bias,
        "ssd_A_log": ssd_A_log,
        "ssd_D": 1.0 + nrm(ks[14], (L, SSD_HEADS), 0.02),
        "ssd_norm_g": 1.0 + nrm(ks[15], (L, SSD_INNER), 0.02),
        "w_branch": w_branch,
        "w_out": nrm(ks[19], (L, D_MODEL, D_MODEL), D_MODEL ** -0.5),
        "norm2_g": 1.0 + nrm(ks[20], (L, D_MODEL), 0.02),
        "w_ffn_in": nrm(ks[21], (L, D_MODEL, 2 * D_FF), D_MODEL ** -0.5),
        "w_ffn_out": nrm(ks[22], (L, D_FF, D_MODEL), D_FF ** -0.5),
        "norm_f": 1.0 + nrm(ks[23], (D_MODEL,), 0.02),
    }


def reference(x, norm1_g, w_in, b_gate, lru_conv_w, lru_conv_b, lru_w_a, lru_b_a, lru_w_x, lru_b_x,
              lru_lambda, ssd_conv_w, ssd_conv_b, ssd_dt_bias, ssd_A_log, ssd_D, ssd_norm_g,
              w_branch, w_out, norm2_g, w_ffn_in, w_ffn_out, norm_f):
    h = x
    for l in range(DEPTH):
        h = h + hybrid_mixer(rms_norm(h, norm1_g[l]), w_in[l], b_gate[l], lru_conv_w[l], lru_conv_b[l],
                             lru_w_a[l], lru_b_a[l], lru_w_x[l], lru_b_x[l], lru_lambda[l],
                             ssd_conv_w[l], ssd_conv_b[l], ssd_dt_bias[l], ssd_A_log[l], ssd_D[l],
                             ssd_norm_g[l], w_branch[l], w_out[l])
        h = h + swiglu(rms_norm(h, norm2_g[l]), w_ffn_in[l], w_ffn_out[l])
    return rms_norm(h, norm_f)
```

```python
import functools

import jax
import jax.numpy as jnp
from jax import lax
from jax.experimental import pallas as pl
from jax.experimental.pallas import tpu as pltpu

F32 = jnp.float32
BF16 = jnp.bfloat16

D_MODEL = 1024
CONV_W = 4
LRU_WIDTH = D_MODEL
LRU_BLOCKS = 16
LRU_BLOCK_DIM = LRU_WIDTH // LRU_BLOCKS
LRU_C = 8.0
SSD_INNER = 2 * D_MODEL
SSD_HEAD_DIM = 64
SSD_HEADS = SSD_INNER // SSD_HEAD_DIM
SSD_GROUPS = 4
SSD_HEADS_PER_GROUP = SSD_HEADS // SSD_GROUPS
SSD_STATE = 128
SSD_BC = SSD_GROUPS * SSD_STATE
SSD_CONV_DIM = SSD_INNER + 2 * SSD_BC
D_FF = 2816
EPS = 1e-6

LANES = 128
SUBLANES = 8
MXU_DIM = 256
VMEM_LIMIT_BYTES = 56 * 1024 * 1024

GROUP_W = SSD_INNER // SSD_GROUPS
PAIRS_PER_GROUP = GROUP_W // LANES
DT_PAD = LANES

PROJ_TM = 256
LRU_T = 256
SSD_L = 128
MERGE_TM = 512
FFN_TM = 512
FFN_CHUNK = D_FF // 2
PROJ_CHUNK = 512
LRU_STRIP = MXU_DIM

PROJ_PIECES = (("lru_x", LRU_WIDTH), ("lru_gate", LRU_WIDTH), ("z", SSD_INNER),
               ("xbc", SSD_CONV_DIM), ("gates", 2 * D_MODEL), ("dt", DT_PAD))
PROJ_COLS = sum(w for _, w in PROJ_PIECES)


def _const_spec(shape):
    nd = len(shape)
    return pl.BlockSpec(shape, lambda *_: (0,) * nd, pipeline_mode=pl.Buffered(1))


def _rms(x, g):
    return x * lax.rsqrt(jnp.mean(x * x, axis=-1, keepdims=True) + EPS) * g


def _proj_kernel(h_ref, g_ref, w_ref, wdt_t_ref,
                 lrux_ref, lrug_ref, z_ref, xbc_ref, gates_ref, dt_ref, dt_t_ref):
    xb = _rms(h_ref[...], g_ref[...]).astype(BF16)
    outs = (lrux_ref, lrug_ref, z_ref, xbc_ref, gates_ref, dt_ref)
    off = 0
    for out_ref, (_, width) in zip(outs, PROJ_PIECES):
        step = min(PROJ_CHUNK, width)
        for c in range(0, width, step):
            out_ref[:, c:c + step] = jnp.dot(
                xb, w_ref[:, off + c:off + c + step], preferred_element_type=F32)
        off += width
    dt_t_ref[...] = lax.dot_general(wdt_t_ref[...], xb, (((1,), (1,)), ((), ())),
                                    preferred_element_type=F32)


def _proj(h, g, w, wdt_t):
    m = h.shape[0]
    row = lambda width: pl.BlockSpec((PROJ_TM, width), lambda i: (i, 0))
    widths = [w_ for _, w_ in PROJ_PIECES]
    return pl.pallas_call(
        _proj_kernel,
        grid=(m // PROJ_TM,),
        in_specs=[row(D_MODEL), _const_spec((1, D_MODEL)),
                  _const_spec((D_MODEL, PROJ_COLS)), _const_spec((SSD_HEADS, D_MODEL))],
        out_specs=[row(w_) for w_ in widths]
        + [pl.BlockSpec((SSD_HEADS, PROJ_TM), lambda i: (0, i))],
        out_shape=[jax.ShapeDtypeStruct((m, w_), F32) for w_ in widths]
        + [jax.ShapeDtypeStruct((SSD_HEADS, m), F32)],
        compiler_params=pltpu.CompilerParams(
            dimension_semantics=("arbitrary",), vmem_limit_bytes=VMEM_LIMIT_BYTES),
        name="proj",
    )(h, g, w, wdt_t)


def _lru_kernel(x_ref, gate_ref, cw_ref, cb_ref, wa_ref, ba_ref, wx_ref, bx_ref, lam_ref,
                out_ref, xpad_ref, a_ref, u_ref, hc_ref):
    t_len, width = x_ref.shape

    @pl.when(pl.program_id(1) == 0)
    def _():
        xpad_ref[0:SUBLANES, :] = jnp.zeros((SUBLANES, width), F32)
        hc_ref[...] = jnp.zeros_like(hc_ref)

    xpad_ref[SUBLANES:SUBLANES + t_len, :] = x_ref[...]
    sp = jax.nn.softplus(-lam_ref[...])
    for q in range(width // LRU_STRIP):
        cs = slice(q * LRU_STRIP, (q + 1) * LRU_STRIP)
        u = cb_ref[:, cs]
        for k in range(CONV_W):
            r0 = SUBLANES - (CONV_W - 1) + k
            u = u + cw_ref[k:k + 1, cs] * xpad_ref[r0:r0 + t_len, cs]
        ub = u.astype(BF16)
        r = jax.nn.sigmoid(jnp.dot(ub, wa_ref[q], preferred_element_type=F32) + ba_ref[:, cs])
        i = jax.nn.sigmoid(jnp.dot(ub, wx_ref[q], preferred_element_type=F32) + bx_ref[:, cs])
        log_a = -LRU_C * r * sp[:, cs]
        a = jnp.exp(log_a)
        a_ref[:, cs] = a
        u_ref[:, cs] = jnp.sqrt(-jnp.tanh(log_a) * (a * a + 1.0)) * (i * u)
    xpad_ref[0:SUBLANES, :] = xpad_ref[t_len:t_len + SUBLANES, :]

    rows = lax.broadcasted_iota(jnp.int32, (SUBLANES, width), 0)

    def body(j, h_prev):
        r0 = pl.multiple_of(j * SUBLANES, SUBLANES)
        a = a_ref[pl.ds(r0, SUBLANES), :]
        b = u_ref[pl.ds(r0, SUBLANES), :]
        for k in (1, 2, 4):
            keep = rows >= k
            a_sh = jnp.where(keep, pltpu.roll(a, k, 0), 1.0)
            b_sh = jnp.where(keep, pltpu.roll(b, k, 0), 0.0)
            b = a * b_sh + b
            a = a * a_sh
        h = a * h_prev + b
        u_ref[pl.ds(r0, SUBLANES), :] = h
        return jnp.broadcast_to(h[SUBLANES - 1:SUBLANES, :], (SUBLANES, width))

    hc_ref[...] = lax.fori_loop(0, t_len // SUBLANES, body, hc_ref[...], unroll=2)
    out_ref[...] = (jax.nn.gelu(gate_ref[...]) * u_ref[...]).astype(BF16)


def _lru(lru_x, lru_gate, cw, cb, wa, ba, wx, bx, lam, batch):
    m, width = lru_x.shape
    nt = m // batch // LRU_T
    row = pl.BlockSpec((LRU_T, width), lambda b, t: (b * nt + t, 0))
    n_strip = width // LRU_STRIP
    return pl.pallas_call(
        _lru_kernel,
        grid=(batch, nt),
        in_specs=[row, row, _const_spec((CONV_W, width)), _const_spec((1, width)),
                  _const_spec((n_strip, LRU_STRIP, LRU_STRIP)), _const_spec((1, width)),
                  _const_spec((n_strip, LRU_STRIP, LRU_STRIP)), _const_spec((1, width)),
                  _const_spec((1, width))],
        out_specs=row,
        out_shape=jax.ShapeDtypeStruct((m, width), BF16),
        scratch_shapes=[pltpu.VMEM((LRU_T + SUBLANES, width), F32),
                        pltpu.VMEM((LRU_T, width), F32),
                        pltpu.VMEM((LRU_T, width), F32),
                        pltpu.VMEM((SUBLANES, width), F32)],
        compiler_params=pltpu.CompilerParams(
            dimension_semantics=("arbitrary", "arbitrary"), vmem_limit_bytes=VMEM_LIMIT_BYTES),
        name="lru",
    )(lru_x, lru_gate, cw, cb, wa, ba, wx, bx, lam)


def _split3(v):
    hi = v.astype(BF16)
    r1 = v - hi.astype(F32)
    mid = r1.astype(BF16)
    lo = (r1 - mid.astype(F32)).astype(BF16)
    return hi, mid, lo


def _ssd_kernel(z_ref, xbc_ref, dt_ref, dt_t_ref, cw_ref, cb_ref, dtb_ref, dtb_t_ref,
                alog_ref, alog_t_ref, dskip_ref, ng_ref, out_ref, xpad_ref, st_ref):
    l_len = z_ref.shape[0]

    @pl.when(pl.program_id(1) == 0)
    def _():
        xpad_ref[0:SUBLANES, :] = jnp.zeros((SUBLANES, SSD_CONV_DIM), F32)
        st_ref[...] = jnp.zeros_like(st_ref)

    xpad_ref[SUBLANES:SUBLANES + l_len, :] = xbc_ref[...]

    def conv_silu(c0, width):
        cs = slice(c0, c0 + width)
        acc = cb_ref[:, cs]
        for k in range(CONV_W):
            r0 = SUBLANES - (CONV_W - 1) + k
            acc = acc + cw_ref[k:k + 1, cs] * xpad_ref[r0:r0 + l_len, cs]
        return jax.nn.silu(acc)

    rows = lax.broadcasted_iota(jnp.int32, (l_len, l_len), 0)
    cols = lax.broadcasted_iota(jnp.int32, (l_len, l_len), 1)
    causal = rows >= cols
    lower = jnp.where(causal, 1.0, 0.0).astype(BF16)
    upper = jnp.where(rows <= cols, 1.0, 0.0).astype(BF16)
    dt = jax.nn.softplus(dt_ref[...] + dtb_ref[...])
    a_parts = _split3(dt * -jnp.exp(alog_ref[...]))
    a_cs = sum(jnp.dot(lower, p, preferred_element_type=F32) for p in reversed(a_parts))
    dt_t = jax.nn.softplus(dt_t_ref[...] + dtb_t_ref[...])
    at_parts = _split3(dt_t * -jnp.exp(alog_t_ref[...]))
    a_cs_t = sum(jnp.dot(p, upper, preferred_element_type=F32) for p in reversed(at_parts))

    lane = lax.broadcasted_iota(jnp.int32, (l_len, LANES), 1)
    first_head = lane < SSD_HEAD_DIM

    def expand(q, head0):
        tiles = []
        for j in range(PAIRS_PER_GROUP):
            h_a = head0 + 2 * j
            tiles.append(jnp.where(first_head, q[:, h_a:h_a + 1], q[:, h_a + 1:h_a + 2]))
        return jnp.concatenate(tiles, axis=1)

    b_all = conv_silu(SSD_INNER, SSD_BC)
    c_all = conv_silu(SSD_INNER + SSD_BC, SSD_BC)

    for g in range(SSD_GROUPS):
        gs = slice(g * GROUP_W, (g + 1) * GROUP_W)
        head0 = g * SSD_HEADS_PER_GROUP
        x_g = conv_silu(g * GROUP_W, GROUP_W)
        b_g = b_all[:, g * SSD_STATE:(g + 1) * SSD_STATE].astype(BF16)
        c_g = c_all[:, g * SSD_STATE:(g + 1) * SSD_STATE].astype(BF16)
        dt_x = expand(dt, head0)
        acs_x = expand(a_cs, head0)
        xdt = x_g * dt_x
        xdt_b = xdt.astype(BF16)
        scores = lax.dot_general(c_g, b_g, (((1,), (1,)), ((), ())),
                                 preferred_element_type=F32)

        y_tiles = []
        for j in range(PAIRS_PER_GROUP):
            x_pair = xdt_b[:, j * LANES:(j + 1) * LANES]
            y_pair = None
            for half in range(2):
                h = head0 + 2 * j + half
                seg = a_cs[:, h:h + 1] - a_cs_t[h:h + 1, :]
                decay = jnp.exp(jnp.where(causal, seg, -jnp.inf))
                m_h = (scores * decay).astype(BF16)
                keep = first_head if half == 0 else jnp.logical_not(first_head)
                x_h = jnp.where(keep, x_pair, jnp.zeros_like(x_pair))
                part = jnp.dot(m_h, x_h, preferred_element_type=F32)
                y_pair = part if y_pair is None else y_pair + part
            y_tiles.append(y_pair)
        y = jnp.concatenate(y_tiles, axis=1)

        e_cs = jnp.exp(acs_x)
        st = st_ref[:, gs]
        y = y + jnp.dot(c_g, st.astype(BF16), preferred_element_type=F32) * e_cs
        last = acs_x[l_len - 1:l_len, :]
        x_end = (xdt * jnp.exp(last - acs_x)).astype(BF16)
        st_ref[:, gs] = st * jnp.exp(last) + lax.dot_general(
            b_g, x_end, (((0,), (0,)), ((), ())), preferred_element_type=F32)

        y = y + dskip_ref[:, gs] * x_g
        y = y * jax.nn.silu(z_ref[:, gs])
        y = y * lax.rsqrt(jnp.mean(y * y, axis=-1, keepdims=True) + EPS)
        out_ref[:, gs] = (y * ng_ref[:, gs]).astype(BF16)

    xpad_ref[0:SUBLANES, :] = xpad_ref[l_len:l_len + SUBLANES, :]


def _ssd(z, xbc, dt, dt_t, cw, cb, dtb, dtb_t, arow, acol, dskip, ng, batch):
    m = z.shape[0]
    nt = m // batch // SSD_L
    row = lambda width: pl.BlockSpec((SSD_L, width), lambda b, t: (b * nt + t, 0))
    return pl.pallas_call(
        _ssd_kernel,
        grid=(batch, nt),
        in_specs=[row(SSD_INNER), row(SSD_CONV_DIM), row(DT_PAD),
                  pl.BlockSpec((SSD_HEADS, SSD_L), lambda b, t: (0, b * nt + t)),
                  _const_spec((CONV_W, SSD_CONV_DIM)), _const_spec((1, SSD_CONV_DIM)),
                  _const_spec((1, DT_PAD)), _const_spec((SSD_HEADS, 1)),
                  _const_spec((1, DT_PAD)), _const_spec((SSD_HEADS, 1)),
                  _const_spec((1, SSD_INNER)), _const_spec((1, SSD_INNER))],
        out_specs=row(SSD_INNER),
        out_shape=jax.ShapeDtypeStruct((m, SSD_INNER), BF16),
        scratch_shapes=[pltpu.VMEM((SSD_L + SUBLANES, SSD_CONV_DIM), F32),
                        pltpu.VMEM((SSD_STATE, SSD_INNER), F32)],
        compiler_params=pltpu.CompilerParams(
            dimension_semantics=("arbitrary", "arbitrary"), vmem_limit_bytes=VMEM_LIMIT_BYTES),
        name="ssd",
    )(z, xbc, dt, dt_t, cw, cb, dtb, dtb_t, arow, acol, dskip, ng)


def _merge_kernel(h_ref, ya_ref, yb_ref, gates_ref, bg_ref, wbr_ref, wout_ref, out_ref):
    g = jax.nn.sigmoid(gates_ref[...] + bg_ref[...])
    p_a = jnp.dot(ya_ref[...], wbr_ref[0:LRU_WIDTH, :], preferred_element_type=F32)
    p_b = jnp.dot(yb_ref[...], wbr_ref[LRU_WIDTH:, :], preferred_element_type=F32)
    merged = g[:, :D_MODEL] * p_a + g[:, D_MODEL:] * p_b
    out_ref[...] = h_ref[...] + jnp.dot(merged.astype(BF16), wout_ref[...],
                                        preferred_element_type=F32)


def _merge(h, y_a, y_b, gates, bg, wbr, wout):
    m = h.shape[0]
    row = lambda width: pl.BlockSpec((MERGE_TM, width), lambda i: (i, 0))
    return pl.pallas_call(
        _merge_kernel,
        grid=(m // MERGE_TM,),
        in_specs=[row(D_MODEL), row(LRU_WIDTH), row(SSD_INNER), row(2 * D_MODEL),
                  _const_spec((1, 2 * D_MODEL)),
                  _const_spec((LRU_WIDTH + SSD_INNER, D_MODEL)),
                  _const_spec((D_MODEL, D_MODEL))],
        out_specs=row(D_MODEL),
        out_shape=jax.ShapeDtypeStruct((m, D_MODEL), F32),
        compiler_params=pltpu.CompilerParams(
            dimension_semantics=("arbitrary",), vmem_limit_bytes=VMEM_LIMIT_BYTES),
        name="merge",
    )(h, y_a, y_b, gates, bg, wbr, wout)


def _ffn_kernel(h_ref, g_ref, w1_ref, w2_ref, gf_ref, out_ref, *, final_norm):
    h = h_ref[...]
    xb = _rms(h, g_ref[...]).astype(BF16)
    acc = h
    for c in range(0, D_FF, FFN_CHUNK):
        gate = jnp.dot(xb, w1_ref[:, c:c + FFN_CHUNK], preferred_element_type=F32)
        up = jnp.dot(xb, w1_ref[:, D_FF + c:D_FF + c + FFN_CHUNK], preferred_element_type=F32)
        act = (jax.nn.silu(gate) * up).astype(BF16)
        acc = acc + jnp.dot(act, w2_ref[c:c + FFN_CHUNK, :], preferred_element_type=F32)
    if final_norm:
        acc = _rms(acc, gf_ref[...])
    out_ref[...] = acc


def _ffn(h, g, w1, w2, gf, final_norm):
    m = h.shape[0]
    row = pl.BlockSpec((FFN_TM, D_MODEL), lambda i: (i, 0))
    return pl.pallas_call(
        functools.partial(_ffn_kernel, final_norm=final_norm),
        grid=(m // FFN_TM,),
        in_specs=[row, _const_spec((1, D_MODEL)), _const_spec((D_MODEL, 2 * D_FF)),
                  _const_spec((D_FF, D_MODEL)), _const_spec((1, D_MODEL))],
        out_specs=row,
        out_shape=jax.ShapeDtypeStruct((m, D_MODEL), F32),
        compiler_params=pltpu.CompilerParams(
            dimension_semantics=("arbitrary",), vmem_limit_bytes=VMEM_LIMIT_BYTES),
        name="ffn",
    )(h, g, w1, w2, gf)


def _block_diag_tiles(w):
    per = LRU_STRIP // LRU_BLOCK_DIM
    w4 = w.reshape(LRU_BLOCKS // per, per, LRU_BLOCK_DIM, LRU_BLOCK_DIM)
    eye = jnp.eye(per, dtype=w.dtype)
    return jnp.einsum("qaij,ab->qaibj", w4, eye).reshape(
        LRU_BLOCKS // per, LRU_STRIP, LRU_STRIP).astype(BF16)


def _pad_lanes(v, width):
    return jnp.pad(v, ((0, 0), (0, width - v.shape[-1])))


def kernel(x, norm1_g, w_in, b_gate, lru_conv_w, lru_conv_b, lru_w_a, lru_b_a, lru_w_x, lru_b_x,
           lru_lambda, ssd_conv_w, ssd_conv_b, ssd_dt_bias, ssd_A_log, ssd_D, ssd_norm_g,
           w_branch, w_out, norm2_g, w_ffn_in, w_ffn_out, norm_f):
    batch, seq, d = x.shape
    depth = w_in.shape[0]
    m = batch * seq
    assert d == D_MODEL and seq % LRU_T == 0 and seq % SSD_L == 0
    assert m % PROJ_TM == 0 and m % MERGE_TM == 0 and m % FFN_TM == 0

    o_lx, o_lg, o_z, o_xbc, o_dt, o_g = (
        0, LRU_WIDTH, 2 * LRU_WIDTH, 2 * LRU_WIDTH + SSD_INNER,
        2 * LRU_WIDTH + SSD_INNER + SSD_CONV_DIM,
        2 * LRU_WIDTH + SSD_INNER + SSD_CONV_DIM + SSD_HEADS)
    row2 = lambda v: v.reshape(1, -1)

    h = x.reshape(m, d)
    for l in range(depth):
        w = w_in[l]
        w_dt = w[:, o_dt:o_g]
        w_r = jnp.concatenate(
            [w[:, o_lx:o_dt], w[:, o_g:], _pad_lanes(w_dt, DT_PAD)], axis=1).astype(BF16)

        lru_x, lru_gate, z, xbc, gates, dt, dt_t = _proj(
            h, row2(norm1_g[l]), w_r, w_dt.T.astype(BF16))
        y_a = _lru(lru_x, lru_gate, lru_conv_w[l], row2(lru_conv_b[l]),
                   _block_diag_tiles(lru_w_a[l]), row2(lru_b_a[l]),
                   _block_diag_tiles(lru_w_x[l]), row2(lru_b_x[l]), row2(lru_lambda[l]), batch)
        y_b = _ssd(z, xbc, dt, dt_t, ssd_conv_w[l], row2(ssd_conv_b[l]),
                   _pad_lanes(row2(ssd_dt_bias[l]), DT_PAD), ssd_dt_bias[l].reshape(-1, 1),
                   _pad_lanes(row2(ssd_A_log[l]), DT_PAD), ssd_A_log[l].reshape(-1, 1),
                   row2(jnp.repeat(ssd_D[l], SSD_HEAD_DIM)), row2(ssd_norm_g[l]), batch)
        h = _merge(h, y_a, y_b, gates, row2(b_gate[l]),
                   w_branch[l].astype(BF16), w_out[l].astype(BF16))
        h = _ffn(h, row2(norm2_g[l]), w_ffn_in[l].astype(BF16), w_ffn_out[l].astype(BF16),
                 row2(norm_f), final_norm=(l == depth - 1))
    return h.reshape(batch, seq, d)
```

```python
import functools

import jax
import jax.numpy as jnp
from jax import lax
from jax.experimental import pallas as pl
from jax.experimental.pallas import tpu as pltpu

F32 = jnp.float32
BF16 = jnp.bfloat16

D_MODEL = 1024
CONV_W = 4
LRU_WIDTH = D_MODEL
LRU_BLOCKS = 16
LRU_BLOCK_DIM = LRU_WIDTH // LRU_BLOCKS
LRU_C = 8.0
SSD_INNER = 2 * D_MODEL
SSD_HEAD_DIM = 64
SSD_HEADS = SSD_INNER // SSD_HEAD_DIM
SSD_GROUPS = 4
SSD_HEADS_PER_GROUP = SSD_HEADS // SSD_GROUPS
SSD_STATE = 128
SSD_BC = SSD_GROUPS * SSD_STATE
SSD_CONV_DIM = SSD_INNER + 2 * SSD_BC
D_FF = 2816
EPS = 1e-6

LANES = 128
SUBLANES = 8
MXU_DIM = 256
VMEM_LIMIT_BYTES = 56 * 1024 * 1024

GROUP_W = SSD_INNER // SSD_GROUPS
PAIRS_PER_GROUP = GROUP_W // LANES
DT_PAD = LANES

MIX_T = 256
SSD_L = 128
FFN_TM = 256
FFN_CHUNK = D_FF // 2
PROJ_CHUNK = 256
LRU_STRIP = MXU_DIM
SCAN_ROWS = 64

P_LRUX = 0
P_LRUG = P_LRUX + LRU_WIDTH
P_Z = P_LRUG + LRU_WIDTH
P_XBC = P_Z + SSD_INNER
P_DT = P_XBC + SSD_CONV_DIM
P_COLS = P_DT + DT_PAD
W_GATES = P_COLS
W_COLS = W_GATES + 2 * D_MODEL


def _const_spec(shape):
    nd = len(shape)
    return pl.BlockSpec(shape, lambda *_: (0,) * nd, pipeline_mode=pl.Buffered(1))


def _rms(x, g):
    return x * lax.rsqrt(jnp.mean(x * x, axis=-1, keepdims=True) + EPS) * g


def _causal_conv(x, tail, cw_ref, cb_ref, cs):
    row = lax.broadcasted_iota(jnp.int32, tail.shape, 0)
    acc = cb_ref[:, cs] + cw_ref[CONV_W - 1:CONV_W, cs] * x
    for s in range(1, CONV_W):
        rolled = pltpu.roll(x, s, 0)
        head = jnp.where(row < s, pltpu.roll(tail, s, 0), rolled[0:SUBLANES, :])
        shifted = jnp.concatenate([head, rolled[SUBLANES:, :]], axis=0)
        acc = acc + cw_ref[CONV_W - 1 - s:CONV_W - s, cs] * shifted
    return acc


def _run_interleaved(main_steps, fill_steps):
    total_main = sum(w for w, _ in main_steps)
    total_fill = sum(w for w, _ in fill_steps)
    done_main = 0.0
    done_fill = 0.0
    fill = list(fill_steps)
    for w, step in main_steps:
        step()
        done_main += w
        while fill and done_fill / total_fill < done_main / total_main:
            fw, fstep = fill.pop(0)
            fstep()
            done_fill += fw
    for _, fstep in fill:
        fstep()


def _lru_steps(ld, cw_ref, cb_ref, wa_ref, ba_ref, wx_ref, bx_ref, lam_ref,
               out_ref, tail_ref, a_ref, u_ref, hc_ref):
    t_len, width = a_ref.shape
    steps = []

    def strip(q):
        cs = slice(q * LRU_STRIP, (q + 1) * LRU_STRIP)
        x = ld(slice(None), slice(P_LRUX + cs.start, P_LRUX + cs.stop))
        u = _causal_conv(x, tail_ref[:, cs], cw_ref, cb_ref, cs)
        ub = u.astype(BF16)
        r = jax.nn.sigmoid(jnp.dot(ub, wa_ref[q], preferred_element_type=F32) + ba_ref[:, cs])
        i = jax.nn.sigmoid(jnp.dot(ub, wx_ref[q], preferred_element_type=F32) + bx_ref[:, cs])
        log_a = -LRU_C * r * jax.nn.softplus(-lam_ref[:, cs])
        a = jnp.exp(log_a)
        a_ref[:, cs] = a
        u_ref[:, cs] = jnp.sqrt(-jnp.tanh(log_a) * (a * a + 1.0)) * (i * u)
        tail_ref[:, cs] = x[t_len - SUBLANES:t_len, :]

    for q in range(width // LRU_STRIP):
        steps.append((500, functools.partial(strip, q)))

    def scan(row0):
        rows = lax.broadcasted_iota(jnp.int32, (SUBLANES, width), 0)
        h_prev = hc_ref[...]
        for j in range(row0 // SUBLANES, (row0 + SCAN_ROWS) // SUBLANES):
            rs = slice(j * SUBLANES, (j + 1) * SUBLANES)
            a = a_ref[rs, :]
            b = u_ref[rs, :]
            for k in (1, 2, 4):
                keep = rows >= k
                a_sh = jnp.where(keep, pltpu.roll(a, k, 0), 1.0)
                b_sh = jnp.where(keep, pltpu.roll(b, k, 0), 0.0)
                b = a * b_sh + b
                a = a * a_sh
            h = a * h_prev + b
            gate = ld(rs, slice(P_LRUG, P_LRUG + width))
            out_ref[rs, :] = (jax.nn.gelu(gate) * h).astype(BF16)
            h_prev = jnp.broadcast_to(h[SUBLANES - 1:SUBLANES, :], (SUBLANES, width))
        hc_ref[...] = h_prev

    for row0 in range(0, t_len, SCAN_ROWS):
        steps.append((480, functools.partial(scan, row0)))
    return steps


def _split3(v):
    hi = v.astype(BF16)
    r1 = v - hi.astype(F32)
    mid = r1.astype(BF16)
    lo = (r1 - mid.astype(F32)).astype(BF16)
    return hi, mid, lo


def _ssd_steps(ld, ld_dt_t, ld_tail, r0, cw_ref, cb_ref, dtb_ref, dtb_t_ref,
               alog_ref, alog_t_ref, dskip_ref, ng_ref, out_ref, st_ref):
    l_len = SSD_L
    rs = slice(r0, r0 + l_len)
    ctx = {}

    def conv_silu(c0, width):
        cs = slice(c0, c0 + width)
        x = ld(rs, slice(P_XBC + c0, P_XBC + c0 + width))
        return jax.nn.silu(_causal_conv(x, ld_tail(cs), cw_ref, cb_ref, cs))

    def prologue():
        rows = lax.broadcasted_iota(jnp.int32, (l_len, l_len), 0)
        cols = lax.broadcasted_iota(jnp.int32, (l_len, l_len), 1)
        causal = rows >= cols
        lower = jnp.where(causal, 1.0, 0.0).astype(BF16)
        upper = jnp.where(rows <= cols, 1.0, 0.0).astype(BF16)
        dt = jax.nn.softplus(ld(rs, slice(P_DT, P_DT + DT_PAD)) + dtb_ref[...])
        a_parts = _split3(dt * -jnp.exp(alog_ref[...]))
        a_cs = sum(jnp.dot(lower, p, preferred_element_type=F32) for p in reversed(a_parts))
        dt_t = jax.nn.softplus(ld_dt_t() + dtb_t_ref[...])
        at_parts = _split3(dt_t * -jnp.exp(alog_t_ref[...]))
        a_cs_t = sum(jnp.dot(p, upper, preferred_element_type=F32) for p in reversed(at_parts))
        lane = lax.broadcasted_iota(jnp.int32, (l_len, LANES), 1)
        ctx.update(causal=causal, dt=dt, a_cs=a_cs, a_cs_t=a_cs_t,
                   first_head=lane < SSD_HEAD_DIM,
                   b_all=conv_silu(SSD_INNER, SSD_BC),
                   c_all=conv_silu(SSD_INNER + SSD_BC, SSD_BC))

    def expand(q, head0):
        tiles = []
        for j in range(PAIRS_PER_GROUP):
            h_a = head0 + 2 * j
            tiles.append(jnp.where(ctx["first_head"], q[:, h_a:h_a + 1], q[:, h_a + 1:h_a + 2]))
        return jnp.concatenate(tiles, axis=1)

    def group_scores(g):
        b_g = ctx["b_all"][:, g * SSD_STATE:(g + 1) * SSD_STATE].astype(BF16)
        c_g = ctx["c_all"][:, g * SSD_STATE:(g + 1) * SSD_STATE].astype(BF16)
        scores = lax.dot_general(c_g, b_g, (((1,), (1,)), ((), ())),
                                 preferred_element_type=F32)
        ctx[g] = (b_g, c_g, scores)

    def group(g):
        causal, a_cs, a_cs_t, first_head = (
            ctx["causal"], ctx["a_cs"], ctx["a_cs_t"], ctx["first_head"])
        b_g, c_g, scores = ctx.pop(g)
        gs = slice(g * GROUP_W, (g + 1) * GROUP_W)
        head0 = g * SSD_HEADS_PER_GROUP
        x_g = conv_silu(g * GROUP_W, GROUP_W)
        dt_x = expand(ctx["dt"], head0)
        acs_x = expand(a_cs, head0)
        xdt = x_g * dt_x
        xdt_b = xdt.astype(BF16)

        y_tiles = []
        for j in range(PAIRS_PER_GROUP):
            x_pair = xdt_b[:, j * LANES:(j + 1) * LANES]
            y_pair = None
            for half in range(2):
                h = head0 + 2 * j + half
                seg = a_cs[:, h:h + 1] - a_cs_t[h:h + 1, :]
                decay = jnp.exp(jnp.where(causal, seg, -jnp.inf))
                m_h = (scores * decay).astype(BF16)
                keep = first_head if half == 0 else jnp.logical_not(first_head)
                x_h = jnp.where(keep, x_pair, jnp.zeros_like(x_pair))
                part = jnp.dot(m_h, x_h, preferred_element_type=F32)
                y_pair = part if y_pair is None else y_pair + part
            y_tiles.append(y_pair)
        y = jnp.concatenate(y_tiles, axis=1)

        e_cs = jnp.exp(acs_x)
        st = st_ref[:, gs]
        y = y + jnp.dot(c_g, st.astype(BF16), preferred_element_type=F32) * e_cs
        last = acs_x[l_len - 1:l_len, :]
        x_end = (xdt * jnp.exp(last - acs_x)).astype(BF16)
        st_ref[:, gs] = st * jnp.exp(last) + lax.dot_general(
            b_g, x_end, (((0,), (0,)), ((), ())), preferred_element_type=F32)

        y = y + dskip_ref[:, gs] * x_g
        y = y * jax.nn.silu(ld(rs, slice(P_Z + gs.start, P_Z + gs.stop)))
        y = y * lax.rsqrt(jnp.mean(y * y, axis=-1, keepdims=True) + EPS)
        out_ref[rs, gs] = (y * ng_ref[:, gs]).astype(BF16)

    steps = [(700, prologue)]
    for g in range(SSD_GROUPS):
        steps += [(1300, functools.partial(group_scores, g)), (0, functools.partial(group, g))]
    return steps


def _mixer_in_kernel(h_ref, g_ref, w_ref, wdt_t_ref,
                     lcw_ref, lcb_ref, wa_ref, ba_ref, wx_ref, bx_ref, lam_ref,
                     scw_ref, scb_ref, dtb_ref, dtb_t_ref, alog_ref, alog_t_ref, dskip_ref, ng_ref,
                     ya_ref, yb_ref, gates_ref,
                     p_ref, dtt_ref, ltail_ref, a_ref, u_ref, hc_ref, stail_ref, st_ref,
                     *, tiles_per_seq):
    i = pl.program_id(0)
    slot = lax.rem(i, 2)
    prev = 1 - slot

    @pl.when(i == 0)
    def _():
        p_ref[1] = jnp.zeros(p_ref.shape[1:], F32)
        dtt_ref[1] = jnp.zeros(dtt_ref.shape[1:], F32)

    @pl.when(jnp.logical_or(i == 0, lax.rem(i + tiles_per_seq - 1, tiles_per_seq) == 0))
    def _():
        ltail_ref[...] = jnp.zeros_like(ltail_ref)
        hc_ref[...] = jnp.zeros_like(hc_ref)
        stail_ref[...] = jnp.zeros_like(stail_ref)
        st_ref[...] = jnp.zeros_like(st_ref)

    xb = _rms(h_ref[...], g_ref[...]).astype(BF16)

    def proj_cols(c, stop):
        p_ref[slot, :, c:stop] = jnp.dot(xb, w_ref[:, c:stop], preferred_element_type=F32)

    def proj_gates(c):
        gates_ref[:, c:c + PROJ_CHUNK] = jnp.dot(
            xb, w_ref[:, W_GATES + c:W_GATES + c + PROJ_CHUNK], preferred_element_type=F32)

    def proj_dt_t():
        dtt_ref[slot] = lax.dot_general(wdt_t_ref[...], xb, (((1,), (1,)), ((), ())),
                                        preferred_element_type=F32)

    proj_steps = [(min(c + PROJ_CHUNK, P_COLS) - c,
                   functools.partial(proj_cols, c, min(c + PROJ_CHUNK, P_COLS)))
                  for c in range(0, P_COLS, PROJ_CHUNK)]
    proj_steps += [(PROJ_CHUNK, functools.partial(proj_gates, c))
                   for c in range(0, 2 * D_MODEL, PROJ_CHUNK)]
    proj_steps.append((SSD_HEADS, proj_dt_t))

    def ld(rows, cols):
        return p_ref[prev, rows, cols]

    mixer_steps = _lru_steps(ld, lcw_ref, lcb_ref, wa_ref, ba_ref, wx_ref, bx_ref, lam_ref,
                             ya_ref, ltail_ref, a_ref, u_ref, hc_ref)
    for r0 in range(0, MIX_T, SSD_L):
        if r0 == 0:
            ld_tail = lambda cs: stail_ref[:, cs]
        else:
            ld_tail = lambda cs, r0=r0: ld(slice(r0 - SUBLANES, r0),
                                           slice(P_XBC + cs.start, P_XBC + cs.stop))
        mixer_steps += _ssd_steps(
            ld, lambda r0=r0: dtt_ref[prev, :, r0:r0 + SSD_L], ld_tail, r0, scw_ref, scb_ref,
            dtb_ref, dtb_t_ref, alog_ref, alog_t_ref, dskip_ref, ng_ref, yb_ref, st_ref)

    def save_tail():
        stail_ref[...] = ld(slice(MIX_T - SUBLANES, MIX_T), slice(P_XBC, P_XBC + SSD_CONV_DIM))

    mixer_steps.append((1, save_tail))
    _run_interleaved(mixer_steps, proj_steps)


def _mixer_in(h, g, w, wdt_t, lru_params, ssd_params, seq):
    m = h.shape[0]
    n = m // MIX_T
    cur = lambda width: pl.BlockSpec((MIX_T, width), lambda i: (jnp.minimum(i, n - 1), 0))
    lag = lambda width: pl.BlockSpec((MIX_T, width), lambda i: (jnp.maximum(i - 1, 0), 0))
    n_strip = LRU_WIDTH // LRU_STRIP
    lru_specs = [_const_spec((CONV_W, LRU_WIDTH)), _const_spec((1, LRU_WIDTH)),
                 _const_spec((n_strip, LRU_STRIP, LRU_STRIP)), _const_spec((1, LRU_WIDTH)),
                 _const_spec((n_strip, LRU_STRIP, LRU_STRIP)), _const_spec((1, LRU_WIDTH)),
                 _const_spec((1, LRU_WIDTH))]
    ssd_specs = [_const_spec((CONV_W, SSD_CONV_DIM)), _const_spec((1, SSD_CONV_DIM)),
                 _const_spec((1, DT_PAD)), _const_spec((SSD_HEADS, 1)),
                 _const_spec((1, DT_PAD)), _const_spec((SSD_HEADS, 1)),
                 _const_spec((1, SSD_INNER)), _const_spec((1, SSD_INNER))]
    return pl.pallas_call(
        functools.partial(_mixer_in_kernel, tiles_per_seq=seq // MIX_T),
        grid=(n + 1,),
        in_specs=[cur(D_MODEL), _const_spec((1, D_MODEL)), _const_spec((D_MODEL, W_COLS)),
                  _const_spec((SSD_HEADS, D_MODEL))] + lru_specs + ssd_specs,
        out_specs=[lag(LRU_WIDTH), lag(SSD_INNER), cur(2 * D_MODEL)],
        out_shape=[jax.ShapeDtypeStruct((m, LRU_WIDTH), BF16),
                   jax.ShapeDtypeStruct((m, SSD_INNER), BF16),
                   jax.ShapeDtypeStruct((m, 2 * D_MODEL), F32)],
        scratch_shapes=[pltpu.VMEM((2, MIX_T, P_COLS), F32),
                        pltpu.VMEM((2, SSD_HEADS, MIX_T), F32),
                        pltpu.VMEM((SUBLANES, LRU_WIDTH), F32),
                        pltpu.VMEM((MIX_T, LRU_WIDTH), F32),
                        pltpu.VMEM((MIX_T, LRU_WIDTH), F32),
                        pltpu.VMEM((SUBLANES, LRU_WIDTH), F32),
                        pltpu.VMEM((SUBLANES, SSD_CONV_DIM), F32),
                        pltpu.VMEM((SSD_STATE, SSD_INNER), F32)],
        compiler_params=pltpu.CompilerParams(
            dimension_semantics=("arbitrary",), vmem_limit_bytes=VMEM_LIMIT_BYTES),
        name="mixer_in",
    )(h, g, w, wdt_t, *lru_params, *ssd_params)


def _merge_ffn_kernel(h_ref, ya_ref, yb_ref, gates_ref, bg_ref, wbr_ref, wout_ref,
                      g2_ref, w1_ref, w2_ref, gf_ref, out_ref, *, final_norm):
    g = jax.nn.sigmoid(gates_ref[...] + bg_ref[...])
    p_a = jnp.dot(ya_ref[...], wbr_ref[0:LRU_WIDTH, :], preferred_element_type=F32)
    p_b = jnp.dot(yb_ref[...], wbr_ref[LRU_WIDTH:, :], preferred_element_type=F32)
    merged = g[:, :D_MODEL] * p_a + g[:, D_MODEL:] * p_b
    h = h_ref[...] + jnp.dot(merged.astype(BF16), wout_ref[...], preferred_element_type=F32)

    xb = _rms(h, g2_ref[...]).astype(BF16)
    acc = h
    for c in range(0, D_FF, FFN_CHUNK):
        gate = jnp.dot(xb, w1_ref[:, c:c + FFN_CHUNK], preferred_element_type=F32)
        up = jnp.dot(xb, w1_ref[:, D_FF + c:D_FF + c + FFN_CHUNK], preferred_element_type=F32)
        act = (jax.nn.silu(gate) * up).astype(BF16)
        acc = acc + jnp.dot(act, w2_ref[c:c + FFN_CHUNK, :], preferred_element_type=F32)
    if final_norm:
        acc = _rms(acc, gf_ref[...])
    out_ref[...] = acc


def _merge_ffn(h, y_a, y_b, gates, bg, wbr, wout, g2, w1, w2, gf, final_norm):
    m = h.shape[0]
    row = lambda width: pl.BlockSpec((FFN_TM, width), lambda i: (i, 0))
    return pl.pallas_call(
        functools.partial(_merge_ffn_kernel, final_norm=final_norm),
        grid=(m // FFN_TM,),
        in_specs=[row(D_MODEL), row(LRU_WIDTH), row(SSD_INNER), row(2 * D_MODEL),
                  _const_spec((1, 2 * D_MODEL)),
                  _const_spec((LRU_WIDTH + SSD_INNER, D_MODEL)),
                  _const_spec((D_MODEL, D_MODEL)),
                  _const_spec((1, D_MODEL)), _const_spec((D_MODEL, 2 * D_FF)),
                  _const_spec((D_FF, D_MODEL)), _const_spec((1, D_MODEL))],
        out_specs=row(D_MODEL),
        out_shape=jax.ShapeDtypeStruct((m, D_MODEL), F32),
        compiler_params=pltpu.CompilerParams(
            dimension_semantics=("arbitrary",), vmem_limit_bytes=VMEM_LIMIT_BYTES),
        name="merge_ffn",
    )(h, y_a, y_b, gates, bg, wbr, wout, g2, w1, w2, gf)


def _block_diag_tiles(w):
    per = LRU_STRIP // LRU_BLOCK_DIM
    w4 = w.reshape(LRU_BLOCKS // per, per, LRU_BLOCK_DIM, LRU_BLOCK_DIM)
    eye = jnp.eye(per, dtype=w.dtype)
    return jnp.einsum("qaij,ab->qaibj", w4, eye).reshape(
        LRU_BLOCKS // per, LRU_STRIP, LRU_STRIP).astype(BF16)


def _pad_lanes(v, width):
    return jnp.pad(v, ((0, 0), (0, width - v.shape[-1])))


def kernel(x, norm1_g, w_in, b_gate, lru_conv_w, lru_conv_b, lru_w_a, lru_b_a, lru_w_x, lru_b_x,
           lru_lambda, ssd_conv_w, ssd_conv_b, ssd_dt_bias, ssd_A_log, ssd_D, ssd_norm_g,
           w_branch, w_out, norm2_g, w_ffn_in, w_ffn_out, norm_f):
    batch, seq, d = x.shape
    depth = w_in.shape[0]
    m = batch * seq
    assert d == D_MODEL and seq % MIX_T == 0 and MIX_T % SSD_L == 0 and m % FFN_TM == 0
    assert MIX_T % SCAN_ROWS == 0

    o_dt = 2 * LRU_WIDTH + SSD_INNER + SSD_CONV_DIM
    o_g = o_dt + SSD_HEADS
    row2 = lambda v: v.reshape(1, -1)

    h = x.reshape(m, d)
    for l in range(depth):
        w = w_in[l]
        w_dt = w[:, o_dt:o_g]
        w_r = jnp.concatenate(
            [w[:, :o_dt], _pad_lanes(w_dt, DT_PAD), w[:, o_g:]], axis=1).astype(BF16)
        lru_params = (lru_conv_w[l], row2(lru_conv_b[l]),
                      _block_diag_tiles(lru_w_a[l]), row2(lru_b_a[l]),
                      _block_diag_tiles(lru_w_x[l]), row2(lru_b_x[l]), row2(lru_lambda[l]))
        ssd_params = (ssd_conv_w[l], row2(ssd_conv_b[l]),
                      _pad_lanes(row2(ssd_dt_bias[l]), DT_PAD), ssd_dt_bias[l].reshape(-1, 1),
                      _pad_lanes(row2(ssd_A_log[l]), DT_PAD), ssd_A_log[l].reshape(-1, 1),
                      row2(jnp.repeat(ssd_D[l], SSD_HEAD_DIM)), row2(ssd_norm_g[l]))
        y_a, y_b, gates = _mixer_in(h, row2(norm1_g[l]), w_r, w_dt.T.astype(BF16),
                                    lru_params, ssd_params, seq)
        h = _merge_ffn(h, y_a, y_b, gates, row2(b_gate[l]),
                       w_branch[l].astype(BF16), w_out[l].astype(BF16),
                       row2(norm2_g[l]), w_ffn_in[l].astype(BF16), w_ffn_out[l].astype(BF16),
                       row2(norm_f), final_norm=(l == depth - 1))
    return h.reshape(batch, seq, d)
```

```python
import functools

import jax
import jax.numpy as jnp
from jax import lax
from jax.experimental import pallas as pl
from jax.experimental.pallas import tpu as pltpu

F32 = jnp.float32
BF16 = jnp.bfloat16

D_MODEL = 1024
CONV_W = 4
LRU_WIDTH = D_MODEL
LRU_BLOCKS = 16
LRU_BLOCK_DIM = LRU_WIDTH // LRU_BLOCKS
LRU_C = 8.0
SSD_INNER = 2 * D_MODEL
SSD_HEAD_DIM = 64
SSD_HEADS = SSD_INNER // SSD_HEAD_DIM
SSD_GROUPS = 4
SSD_HEADS_PER_GROUP = SSD_HEADS // SSD_GROUPS
SSD_STATE = 128
SSD_BC = SSD_GROUPS * SSD_STATE
SSD_CONV_DIM = SSD_INNER + 2 * SSD_BC
D_FF = 2816
EPS = 1e-6

LANES = 128
SUBLANES = 8
MXU_DIM = 256
VMEM_LIMIT_BYTES = 56 * 1024 * 1024

GROUP_W = SSD_INNER // SSD_GROUPS
PAIRS_PER_GROUP = GROUP_W // LANES
DT_PAD = LANES

MIX_T = 256
SSD_L = 128
FFN_TM = 256
FFN_CHUNK = D_FF // 2
PROJ_CHUNK = 256
LRU_STRIP = MXU_DIM

CHUNK_BLOCKS = SSD_L // SUBLANES
TAIL_ROWS = (CONV_W - 1) * SUBLANES

P_LRUX = 0
P_LRUG = P_LRUX + LRU_WIDTH
P_Z = P_LRUG + LRU_WIDTH
P_XBC = P_Z + SSD_INNER
P_DT = P_XBC + SSD_CONV_DIM
P_COLS = P_DT + DT_PAD
W_GATES = P_COLS
W_COLS = W_GATES + 2 * D_MODEL


def _const_spec(shape):
    nd = len(shape)
    return pl.BlockSpec(shape, lambda *_: (0,) * nd, pipeline_mode=pl.Buffered(1))


def _rms(x, g):
    return x * lax.rsqrt(jnp.mean(x * x, axis=-1, keepdims=True) + EPS) * g


def _causal_conv(x, tail, cw_ref, cb_ref, cs):
    sub = lax.broadcasted_iota(jnp.int32, (SUBLANES, x.shape[1]), 0)
    tiles = [x[j * SUBLANES:(j + 1) * SUBLANES, :] for j in range(CHUNK_BLOCKS)]
    wrapped = []
    for m in range(CONV_W - 1):
        cur = pltpu.roll(tiles[CHUNK_BLOCKS - (CONV_W - 1) + m], 1, 0)
        prv = pltpu.roll(tail[m * SUBLANES:(m + 1) * SUBLANES, :], 1, 0)
        wrapped.append(jnp.where(sub == 0, prv, cur))
    ext = wrapped + tiles
    acc = cb_ref[:, cs] + cw_ref[CONV_W - 1:CONV_W, cs] * x
    for k in range(1, CONV_W):
        shifted = jnp.concatenate(ext[CONV_W - 1 - k:CONV_W - 1 - k + CHUNK_BLOCKS], axis=0)
        acc = acc + cw_ref[CONV_W - 1 - k:CONV_W - k, cs] * shifted
    return acc


def _run_interleaved(main_steps, fill_steps):
    total_main = sum(s[0] for s in main_steps)
    total_fill = sum(w for w, _ in fill_steps)
    done_main = 0.0
    done_fill = 0.0
    fill = list(fill_steps)
    pending_post = None
    for w, pre, mm, post in main_steps:
        if pre is not None:
            pre()
        if pending_post is not None:
            pending_post()
        done_main += w
        while fill and done_fill / total_fill < done_main / total_main:
            fw, fstep = fill.pop(0)
            fstep()
            done_fill += fw
        if mm is not None:
            mm()
        pending_post = post
    if pending_post is not None:
        pending_post()
    for _, fstep in fill:
        fstep()


def _lru_steps(ld, cw_ref, cb_ref, wa_ref, ba_ref, wx_ref, bx_ref, lam_ref,
               out_ref, tail_ref, a_ref, u_ref, hc_ref):
    t_len, width = a_ref.shape
    steps = []

    ctx = {}

    def strip_pre(q):
        cs = slice(q * LRU_STRIP, (q + 1) * LRU_STRIP)
        pcs = slice(P_LRUX + cs.start, P_LRUX + cs.stop)
        tail = tail_ref[:, cs]
        convs = []
        for r0 in range(0, t_len, SSD_L):
            x = ld(slice(r0, r0 + SSD_L), pcs)
            convs.append(_causal_conv(x, tail, cw_ref, cb_ref, cs))
            tail = x[SSD_L - TAIL_ROWS:SSD_L, :]
        tail_ref[:, cs] = tail
        u = jnp.concatenate(convs, axis=0)
        ctx[q] = (u, u.astype(BF16))

    def strip_mm(q):
        u, ub = ctx[q]
        ctx[q] = (u, jnp.dot(ub, wa_ref[q], preferred_element_type=F32),
                  jnp.dot(ub, wx_ref[q], preferred_element_type=F32))

    def strip_post(q):
        cs = slice(q * LRU_STRIP, (q + 1) * LRU_STRIP)
        u, r_lin, i_lin = ctx.pop(q)
        r = jax.nn.sigmoid(r_lin + ba_ref[:, cs])
        i = jax.nn.sigmoid(i_lin + bx_ref[:, cs])
        log_a = -LRU_C * r * jax.nn.softplus(-lam_ref[:, cs])
        a = jnp.exp(log_a)
        a_ref[:, cs] = a
        u_ref[:, cs] = jnp.sqrt(-jnp.tanh(log_a) * (a * a + 1.0)) * (i * u)

    for q in range(width // LRU_STRIP):
        steps.append((525, functools.partial(strip_pre, q), functools.partial(strip_mm, q),
                      functools.partial(strip_post, q)))

    def scan(r0):
        sub = lax.broadcasted_iota(jnp.int32, (SUBLANES, width), 0)
        tile_rows = lambda j: slice(r0 + j * SUBLANES, r0 + (j + 1) * SUBLANES)
        h = p = None
        for j in range(CHUNK_BLOCKS):
            a = a_ref[tile_rows(j), :]
            b = u_ref[tile_rows(j), :]
            h = b if j == 0 else a * h + b
            p = a if j == 0 else a * p
            u_ref[tile_rows(j), :] = h
            a_ref[tile_rows(j), :] = p
        for k in (1, 2, 4):
            keep = sub >= k
            p_sh = jnp.where(keep, pltpu.roll(p, k, 0), 1.0)
            h_sh = jnp.where(keep, pltpu.roll(h, k, 0), 0.0)
            h = p * h_sh + h
            p = p * p_sh
        h0 = hc_ref[...]
        ends = h + p * h0
        carry = jnp.where(sub == 0, h0, pltpu.roll(ends, 1, 0))
        hc_ref[...] = jnp.broadcast_to(ends[SUBLANES - 1:SUBLANES, :], (SUBLANES, width))
        for j in range(CHUNK_BLOCKS):
            h_true = u_ref[tile_rows(j), :] + a_ref[tile_rows(j), :] * carry
            gate = ld(tile_rows(j), slice(P_LRUG, P_LRUG + width))
            out_ref[tile_rows(j), :] = (jax.nn.gelu(gate) * h_true).astype(BF16)

    for r0 in range(0, t_len, SSD_L):
        steps.append((500, None, None, functools.partial(scan, r0)))
    return steps


def _split3(v):
    hi = v.astype(BF16)
    r1 = v - hi.astype(F32)
    mid = r1.astype(BF16)
    lo = (r1 - mid.astype(F32)).astype(BF16)
    return hi, mid, lo


def _ssd_steps(ld, ld_dt_t, ld_tail, r0, cw_ref, cb_ref, dtb_ref, dtb_t_ref,
               alog_ref, alog_t_ref, dskip_ref, ng_ref, out_ref, st_ref):
    l_len = SSD_L
    rs = slice(r0, r0 + l_len)
    ctx = {}

    def conv_silu(c0, width):
        cs = slice(c0, c0 + width)
        x = ld(rs, slice(P_XBC + c0, P_XBC + c0 + width))
        return jax.nn.silu(_causal_conv(x, ld_tail(cs), cw_ref, cb_ref, cs))

    def prologue_pre():
        def time_of(axis):
            r = lax.broadcasted_iota(jnp.int32, (l_len, l_len), axis)
            return (r % SUBLANES) * CHUNK_BLOCKS + r // SUBLANES

        rows = time_of(0)
        cols = time_of(1)
        causal = rows >= cols
        dt = jax.nn.softplus(ld(rs, slice(P_DT, P_DT + DT_PAD)) + dtb_ref[...])
        dt_t = jax.nn.softplus(ld_dt_t() + dtb_t_ref[...])
        lane = lax.broadcasted_iota(jnp.int32, (l_len, LANES), 1)
        ctx.update(
            causal=causal, dt=dt, first_head=lane < SSD_HEAD_DIM,
            lower=jnp.where(causal, 1.0, 0.0).astype(BF16),
            upper=jnp.where(rows <= cols, 1.0, 0.0).astype(BF16),
            a_parts=_split3(dt * -jnp.exp(alog_ref[...])),
            at_parts=_split3(dt_t * -jnp.exp(alog_t_ref[...])),
            b_all=conv_silu(SSD_INNER, SSD_BC).astype(BF16),
            c_all=conv_silu(SSD_INNER + SSD_BC, SSD_BC).astype(BF16))

    def scores_mm(g):
        b_g = ctx["b_all"][:, g * SSD_STATE:(g + 1) * SSD_STATE]
        c_g = ctx["c_all"][:, g * SSD_STATE:(g + 1) * SSD_STATE]
        scores = lax.dot_general(c_g, b_g, (((1,), (1,)), ((), ())),
                                 preferred_element_type=F32)
        ctx[g] = dict(b_g=b_g, c_g=c_g, scores=scores)

    def prologue_mm():
        lower, upper = ctx.pop("lower"), ctx.pop("upper")
        ctx["a_cs"] = sum(jnp.dot(lower, p, preferred_element_type=F32)
                          for p in reversed(ctx.pop("a_parts")))
        ctx["a_cs_t"] = sum(jnp.dot(p, upper, preferred_element_type=F32)
                            for p in reversed(ctx.pop("at_parts")))
        scores_mm(0)

    def expand(q, head0):
        tiles = []
        for j in range(PAIRS_PER_GROUP):
            h_a = head0 + 2 * j
            tiles.append(jnp.where(ctx["first_head"], q[:, h_a:h_a + 1], q[:, h_a + 1:h_a + 2]))
        return jnp.concatenate(tiles, axis=1)

    def group_pre(g):
        causal, a_cs, a_cs_t, first_head = (
            ctx["causal"], ctx["a_cs"], ctx["a_cs_t"], ctx["first_head"])
        c = ctx[g]
        gs = slice(g * GROUP_W, (g + 1) * GROUP_W)
        head0 = g * SSD_HEADS_PER_GROUP
        x_g = conv_silu(g * GROUP_W, GROUP_W)
        acs_x = expand(a_cs, head0)
        xdt = x_g * expand(ctx["dt"], head0)
        xdt_b = xdt.astype(BF16)
        lhs, rhs = [], []
        for j in range(PAIRS_PER_GROUP):
            x_pair = xdt_b[:, j * LANES:(j + 1) * LANES]
            for half in range(2):
                h = head0 + 2 * j + half
                seg = a_cs[:, h:h + 1] - a_cs_t[h:h + 1, :]
                decay = jnp.exp(jnp.where(causal, seg, -jnp.inf))
                lhs.append((c["scores"] * decay).astype(BF16))
                keep = first_head if half == 0 else jnp.logical_not(first_head)
                rhs.append(jnp.where(keep, x_pair, jnp.zeros_like(x_pair)))
        last = acs_x[l_len - 1:l_len, :]
        st = st_ref[:, gs]
        c.update(x_g=x_g, lhs=lhs, rhs=rhs, e_cs=jnp.exp(acs_x), st_b=st.astype(BF16),
                 st_decayed=st * jnp.exp(last),
                 x_end=(xdt * jnp.exp(last - acs_x)).astype(BF16))

    def group_mm(g):
        c = ctx[g]
        lhs, rhs = c.pop("lhs"), c.pop("rhs")
        y_tiles = []
        for j in range(PAIRS_PER_GROUP):
            y_tiles.append(jnp.dot(lhs[2 * j], rhs[2 * j], preferred_element_type=F32)
                           + jnp.dot(lhs[2 * j + 1], rhs[2 * j + 1], preferred_element_type=F32))
        c["y_diag"] = jnp.concatenate(y_tiles, axis=1)
        c["y_off"] = jnp.dot(c["c_g"], c.pop("st_b"), preferred_element_type=F32)
        c["st_new"] = lax.dot_general(c["b_g"], c.pop("x_end"), (((0,), (0,)), ((), ())),
                                      preferred_element_type=F32)
        if g + 1 < SSD_GROUPS:
            scores_mm(g + 1)

    def group_post(g):
        c = ctx.pop(g)
        gs = slice(g * GROUP_W, (g + 1) * GROUP_W)
        st_ref[:, gs] = c["st_decayed"] + c["st_new"]
        y = c["y_diag"] + c["y_off"] * c["e_cs"] + dskip_ref[:, gs] * c["x_g"]
        y = y * jax.nn.silu(ld(rs, slice(P_Z + gs.start, P_Z + gs.stop)))
        y = y * lax.rsqrt(jnp.mean(y * y, axis=-1, keepdims=True) + EPS)
        out_ref[rs, gs] = (y * ng_ref[:, gs]).astype(BF16)

    steps = [(600, prologue_pre, prologue_mm, None)]
    for g in range(SSD_GROUPS):
        steps.append((950, functools.partial(group_pre, g), functools.partial(group_mm, g),
                      functools.partial(group_post, g)))
    return steps


def _mixer_in_kernel(h_ref, g_ref, w_ref, wdt_t_ref,
                     lcw_ref, lcb_ref, wa_ref, ba_ref, wx_ref, bx_ref, lam_ref,
                     scw_ref, scb_ref, dtb_ref, dtb_t_ref, alog_ref, alog_t_ref, dskip_ref, ng_ref,
                     ya_ref, yb_ref, gates_ref,
                     p_ref, dtt_ref, ltail_ref, a_ref, u_ref, hc_ref, stail_ref, st_ref,
                     *, tiles_per_seq):
    i = pl.program_id(0)
    slot = lax.rem(i, 2)
    prev = 1 - slot

    @pl.when(i == 0)
    def _():
        p_ref[1] = jnp.zeros(p_ref.shape[1:], F32)
        dtt_ref[1] = jnp.zeros(dtt_ref.shape[1:], F32)

    @pl.when(jnp.logical_or(i == 0, lax.rem(i + tiles_per_seq - 1, tiles_per_seq) == 0))
    def _():
        ltail_ref[...] = jnp.zeros_like(ltail_ref)
        hc_ref[...] = jnp.zeros_like(hc_ref)
        stail_ref[...] = jnp.zeros_like(stail_ref)
        st_ref[...] = jnp.zeros_like(st_ref)

    xb = _rms(h_ref[...], g_ref[...]).astype(BF16)

    def proj_cols(c, stop):
        p_ref[slot, :, c:stop] = jnp.dot(xb, w_ref[:, c:stop], preferred_element_type=F32)

    def proj_gates(c):
        gates_ref[:, c:c + PROJ_CHUNK] = jnp.dot(
            xb, w_ref[:, W_GATES + c:W_GATES + c + PROJ_CHUNK], preferred_element_type=F32)

    def proj_dt_t():
        dtt_ref[slot] = lax.dot_general(wdt_t_ref[...], xb, (((1,), (1,)), ((), ())),
                                        preferred_element_type=F32)

    proj_steps = [(min(c + PROJ_CHUNK, P_COLS) - c,
                   functools.partial(proj_cols, c, min(c + PROJ_CHUNK, P_COLS)))
                  for c in range(0, P_COLS, PROJ_CHUNK)]
    proj_steps += [(PROJ_CHUNK, functools.partial(proj_gates, c))
                   for c in range(0, 2 * D_MODEL, PROJ_CHUNK)]
    proj_steps.append((SSD_HEADS, proj_dt_t))

    def ld(rows, cols):
        return p_ref[prev, rows, cols]

    mixer_steps = _lru_steps(ld, lcw_ref, lcb_ref, wa_ref, ba_ref, wx_ref, bx_ref, lam_ref,
                             ya_ref, ltail_ref, a_ref, u_ref, hc_ref)
    for r0 in range(0, MIX_T, SSD_L):
        if r0 == 0:
            ld_tail = lambda cs: stail_ref[:, cs]
        else:
            ld_tail = lambda cs, r0=r0: ld(slice(r0 - TAIL_ROWS, r0),
                                           slice(P_XBC + cs.start, P_XBC + cs.stop))
        mixer_steps += _ssd_steps(
            ld, lambda r0=r0: dtt_ref[prev, :, r0:r0 + SSD_L], ld_tail, r0, scw_ref, scb_ref,
            dtb_ref, dtb_t_ref, alog_ref, alog_t_ref, dskip_ref, ng_ref, yb_ref, st_ref)

    def save_tail():
        stail_ref[...] = ld(slice(MIX_T - TAIL_ROWS, MIX_T), slice(P_XBC, P_XBC + SSD_CONV_DIM))

    mixer_steps.append((1, None, None, save_tail))
    _run_interleaved(mixer_steps, proj_steps)


def _mixer_in(h, g, w, wdt_t, lru_params, ssd_params, seq):
    m = h.shape[0]
    n = m // MIX_T
    cur = lambda width: pl.BlockSpec((MIX_T, width), lambda i: (jnp.minimum(i, n - 1), 0))
    lag = lambda width: pl.BlockSpec((MIX_T, width), lambda i: (jnp.maximum(i - 1, 0), 0))
    n_strip = LRU_WIDTH // LRU_STRIP
    lru_specs = [_const_spec((CONV_W, LRU_WIDTH)), _const_spec((1, LRU_WIDTH)),
                 _const_spec((n_strip, LRU_STRIP, LRU_STRIP)), _const_spec((1, LRU_WIDTH)),
                 _const_spec((n_strip, LRU_STRIP, LRU_STRIP)), _const_spec((1, LRU_WIDTH)),
                 _const_spec((1, LRU_WIDTH))]
    ssd_specs = [_const_spec((CONV_W, SSD_CONV_DIM)), _const_spec((1, SSD_CONV_DIM)),
                 _const_spec((1, DT_PAD)), _const_spec((SSD_HEADS, 1)),
                 _const_spec((1, DT_PAD)), _const_spec((SSD_HEADS, 1)),
                 _const_spec((1, SSD_INNER)), _const_spec((1, SSD_INNER))]
    return pl.pallas_call(
        functools.partial(_mixer_in_kernel, tiles_per_seq=seq // MIX_T),
        grid=(n + 1,),
        in_specs=[cur(D_MODEL), _const_spec((1, D_MODEL)), _const_spec((D_MODEL, W_COLS)),
                  _const_spec((SSD_HEADS, D_MODEL))] + lru_specs + ssd_specs,
        out_specs=[lag(LRU_WIDTH), lag(SSD_INNER), cur(2 * D_MODEL)],
        out_shape=[jax.ShapeDtypeStruct((m, LRU_WIDTH), BF16),
                   jax.ShapeDtypeStruct((m, SSD_INNER), BF16),
                   jax.ShapeDtypeStruct((m, 2 * D_MODEL), F32)],
        scratch_shapes=[pltpu.VMEM((2, MIX_T, P_COLS), F32),
                        pltpu.VMEM((2, SSD_HEADS, MIX_T), F32),
                        pltpu.VMEM((TAIL_ROWS, LRU_WIDTH), F32),
                        pltpu.VMEM((MIX_T, LRU_WIDTH), F32),
                        pltpu.VMEM((MIX_T, LRU_WIDTH), F32),
                        pltpu.VMEM((SUBLANES, LRU_WIDTH), F32),
                        pltpu.VMEM((TAIL_ROWS, SSD_CONV_DIM), F32),
                        pltpu.VMEM((SSD_STATE, SSD_INNER), F32)],
        compiler_params=pltpu.CompilerParams(
            dimension_semantics=("arbitrary",), vmem_limit_bytes=VMEM_LIMIT_BYTES),
        name="mixer_in",
    )(h, g, w, wdt_t, *lru_params, *ssd_params)


def _merge_ffn_kernel(h_ref, ya_ref, yb_ref, gates_ref, bg_ref, wbr_ref, wout_ref,
                      g2_ref, w1_ref, w2_ref, gf_ref, out_ref, *, final_norm):
    g = jax.nn.sigmoid(gates_ref[...] + bg_ref[...])
    p_a = jnp.dot(ya_ref[...], wbr_ref[0:LRU_WIDTH, :], preferred_element_type=F32)
    p_b = jnp.dot(yb_ref[...], wbr_ref[LRU_WIDTH:, :], preferred_element_type=F32)
    merged = g[:, :D_MODEL] * p_a + g[:, D_MODEL:] * p_b
    h = h_ref[...] + jnp.dot(merged.astype(BF16), wout_ref[...], preferred_element_type=F32)

    xb = _rms(h, g2_ref[...]).astype(BF16)
    acc = h
    for c in range(0, D_FF, FFN_CHUNK):
        gate = jnp.dot(xb, w1_ref[:, c:c + FFN_CHUNK], preferred_element_type=F32)
        up = jnp.dot(xb, w1_ref[:, D_FF + c:D_FF + c + FFN_CHUNK], preferred_element_type=F32)
        act = (jax.nn.silu(gate) * up).astype(BF16)
        acc = acc + jnp.dot(act, w2_ref[c:c + FFN_CHUNK, :], preferred_element_type=F32)
    if final_norm:
        acc = _rms(acc, gf_ref[...])
    out_ref[...] = acc


def _merge_ffn(h, y_a, y_b, gates, bg, wbr, wout, g2, w1, w2, gf, final_norm):
    m = h.shape[0]
    row = lambda width: pl.BlockSpec((FFN_TM, width), lambda i: (i, 0))
    return pl.pallas_call(
        functools.partial(_merge_ffn_kernel, final_norm=final_norm),
        grid=(m // FFN_TM,),
        in_specs=[row(D_MODEL), row(LRU_WIDTH), row(SSD_INNER), row(2 * D_MODEL),
                  _const_spec((1, 2 * D_MODEL)),
                  _const_spec((LRU_WIDTH + SSD_INNER, D_MODEL)),
                  _const_spec((D_MODEL, D_MODEL)),
                  _const_spec((1, D_MODEL)), _const_spec((D_MODEL, 2 * D_FF)),
                  _const_spec((D_FF, D_MODEL)), _const_spec((1, D_MODEL))],
        out_specs=row(D_MODEL),
        out_shape=jax.ShapeDtypeStruct((m, D_MODEL), F32),
        compiler_params=pltpu.CompilerParams(
            dimension_semantics=("arbitrary",), vmem_limit_bytes=VMEM_LIMIT_BYTES),
        name="merge_ffn",
    )(h, y_a, y_b, gates, bg, wbr, wout, g2, w1, w2, gf)


def _block_diag_tiles(w):
    per = LRU_STRIP // LRU_BLOCK_DIM
    w4 = w.reshape(LRU_BLOCKS // per, per, LRU_BLOCK_DIM, LRU_BLOCK_DIM)
    eye = jnp.eye(per, dtype=w.dtype)
    return jnp.einsum("qaij,ab->qaibj", w4, eye).reshape(
        LRU_BLOCKS // per, LRU_STRIP, LRU_STRIP).astype(BF16)


def _pad_lanes(v, width):
    return jnp.pad(v, ((0, 0), (0, width - v.shape[-1])))


def kernel(x, norm1_g, w_in, b_gate, lru_conv_w, lru_conv_b, lru_w_a, lru_b_a, lru_w_x, lru_b_x,
           lru_lambda, ssd_conv_w, ssd_conv_b, ssd_dt_bias, ssd_A_log, ssd_D, ssd_norm_g,
           w_branch, w_out, norm2_g, w_ffn_in, w_ffn_out, norm_f):
    batch, seq, d = x.shape
    depth = w_in.shape[0]
    m = batch * seq
    assert d == D_MODEL and seq % MIX_T == 0 and MIX_T % SSD_L == 0 and m % FFN_TM == 0

    o_dt = 2 * LRU_WIDTH + SSD_INNER + SSD_CONV_DIM
    o_g = o_dt + SSD_HEADS
    row2 = lambda v: v.reshape(1, -1)

    h = x.reshape(m // SSD_L, SUBLANES, CHUNK_BLOCKS, d).swapaxes(1, 2).reshape(m, d)
    for l in range(depth):
        w = w_in[l]
        w_dt = w[:, o_dt:o_g]
        w_r = jnp.concatenate(
            [w[:, :o_dt], _pad_lanes(w_dt, DT_PAD), w[:, o_g:]], axis=1).astype(BF16)
        lru_params = (lru_conv_w[l], row2(lru_conv_b[l]),
                      _block_diag_tiles(lru_w_a[l]), row2(lru_b_a[l]),
                      _block_diag_tiles(lru_w_x[l]), row2(lru_b_x[l]), row2(lru_lambda[l]))
        ssd_params = (ssd_conv_w[l], row2(ssd_conv_b[l]),
                      _pad_lanes(row2(ssd_dt_bias[l]), DT_PAD), ssd_dt_bias[l].reshape(-1, 1),
                      _pad_lanes(row2(ssd_A_log[l]), DT_PAD), ssd_A_log[l].reshape(-1, 1),
                      row2(jnp.repeat(ssd_D[l], SSD_HEAD_DIM)), row2(ssd_norm_g[l]))
        y_a, y_b, gates = _mixer_in(h, row2(norm1_g[l]), w_r, w_dt.T.astype(BF16),
                                    lru_params, ssd_params, seq)
        h = _merge_ffn(h, y_a, y_b, gates, row2(b_gate[l]),
                       w_branch[l].astype(BF16), w_out[l].astype(BF16),
                       row2(norm2_g[l]), w_ffn_in[l].astype(BF16), w_ffn_out[l].astype(BF16),
                       row2(norm_f), final_norm=(l == depth - 1))
    h = h.reshape(m // SSD_L, CHUNK_BLOCKS, SUBLANES, d).swapaxes(1, 2)
    return h.reshape(batch, seq, d)
```

```python
import functools

import jax
import jax.numpy as jnp
from jax import lax
from jax.experimental import pallas as pl
from jax.experimental.pallas import tpu as pltpu

F32 = jnp.float32
BF16 = jnp.bfloat16

D_MODEL = 1024
CONV_W = 4
LRU_WIDTH = D_MODEL
LRU_BLOCKS = 16
LRU_BLOCK_DIM = LRU_WIDTH // LRU_BLOCKS
LRU_C = 8.0
SSD_INNER = 2 * D_MODEL
SSD_HEAD_DIM = 64
SSD_HEADS = SSD_INNER // SSD_HEAD_DIM
SSD_GROUPS = 4
SSD_HEADS_PER_GROUP = SSD_HEADS // SSD_GROUPS
SSD_STATE = 128
SSD_BC = SSD_GROUPS * SSD_STATE
SSD_CONV_DIM = SSD_INNER + 2 * SSD_BC
D_FF = 2816
EPS = 1e-6

LANES = 128
SUBLANES = 8
MXU_DIM = 256
VMEM_LIMIT_BYTES = 56 * 1024 * 1024

GROUP_W = SSD_INNER // SSD_GROUPS
PAIRS_PER_GROUP = GROUP_W // LANES
DT_PAD = LANES

MIX_T = 256
SSD_L = 128
FFN_TM = 256
FFN_CHUNK = D_FF // 2
PROJ_CHUNK = 256
LRU_STRIP = MXU_DIM

CHUNK_BLOCKS = SSD_L // SUBLANES
TAIL_ROWS = (CONV_W - 1) * SUBLANES

P_LRUX = 0
P_LRUG = P_LRUX + LRU_WIDTH
P_Z = P_LRUG + LRU_WIDTH
P_XBC = P_Z + SSD_INNER
P_DT = P_XBC + SSD_CONV_DIM
P_COLS = P_DT + DT_PAD
W_GATES = P_COLS
W_COLS = W_GATES + 2 * D_MODEL


def _const_spec(shape):
    nd = len(shape)
    return pl.BlockSpec(shape, lambda *_: (0,) * nd, pipeline_mode=pl.Buffered(1))


def _layer_spec(shape, layer):
    nd = len(shape)
    return pl.BlockSpec((None,) + shape, lambda *_: (layer,) + (0,) * nd,
                        pipeline_mode=pl.Buffered(1))


def _rms(x, g):
    return x * lax.rsqrt(jnp.mean(x * x, axis=-1, keepdims=True) + EPS) * g


def _causal_conv(x, tail, cw_ref, cb_ref, cs):
    sub = lax.broadcasted_iota(jnp.int32, (SUBLANES, x.shape[1]), 0)
    tiles = [x[j * SUBLANES:(j + 1) * SUBLANES, :] for j in range(CHUNK_BLOCKS)]
    wrapped = []
    for m in range(CONV_W - 1):
        cur = pltpu.roll(tiles[CHUNK_BLOCKS - (CONV_W - 1) + m], 1, 0)
        prv = pltpu.roll(tail[m * SUBLANES:(m + 1) * SUBLANES, :], 1, 0)
        wrapped.append(jnp.where(sub == 0, prv, cur))
    ext = wrapped + tiles
    acc = cb_ref[:, cs] + cw_ref[CONV_W - 1:CONV_W, cs] * x
    for k in range(1, CONV_W):
        shifted = jnp.concatenate(ext[CONV_W - 1 - k:CONV_W - 1 - k + CHUNK_BLOCKS], axis=0)
        acc = acc + cw_ref[CONV_W - 1 - k:CONV_W - k, cs] * shifted
    return acc


def _run_interleaved(main_steps, fill_steps):
    total_main = sum(s[0] for s in main_steps)
    total_fill = sum(w for w, _ in fill_steps)
    done_main = 0.0
    done_fill = 0.0
    fill = list(fill_steps)
    pending_post = None
    for w, pre, mm, post in main_steps:
        if pre is not None:
            pre()
        if pending_post is not None:
            pending_post()
        done_main += w
        while fill and done_fill / total_fill < done_main / total_main:
            fw, fstep = fill.pop(0)
            fstep()
            done_fill += fw
        if mm is not None:
            mm()
        pending_post = post
    if pending_post is not None:
        pending_post()
    for _, fstep in fill:
        fstep()


def _lru_steps(ld, cw_ref, cb_ref, wa_ref, ba_ref, wx_ref, bx_ref, lam_ref,
               out_ref, tail_ref, a_ref, u_ref, hc_ref):
    t_len, width = a_ref.shape
    steps = []

    ctx = {}

    def strip_pre(q):
        cs = slice(q * LRU_STRIP, (q + 1) * LRU_STRIP)
        pcs = slice(P_LRUX + cs.start, P_LRUX + cs.stop)
        tail = tail_ref[:, cs]
        convs = []
        for r0 in range(0, t_len, SSD_L):
            x = ld(slice(r0, r0 + SSD_L), pcs)
            convs.append(_causal_conv(x, tail, cw_ref, cb_ref, cs))
            tail = x[SSD_L - TAIL_ROWS:SSD_L, :]
        tail_ref[:, cs] = tail
        u = jnp.concatenate(convs, axis=0)
        ctx[q] = (u, u.astype(BF16))

    def strip_mm(q):
        u, ub = ctx[q]
        ctx[q] = (u, jnp.dot(ub, wa_ref[q], preferred_element_type=F32),
                  jnp.dot(ub, wx_ref[q], preferred_element_type=F32))

    def strip_post(q):
        cs = slice(q * LRU_STRIP, (q + 1) * LRU_STRIP)
        u, r_lin, i_lin = ctx.pop(q)
        r = jax.nn.sigmoid(r_lin + ba_ref[:, cs])
        i = jax.nn.sigmoid(i_lin + bx_ref[:, cs])
        log_a = -LRU_C * r * jax.nn.softplus(-lam_ref[:, cs])
        a = jnp.exp(log_a)
        a_ref[:, cs] = a
        u_ref[:, cs] = jnp.sqrt(-jnp.tanh(log_a) * (a * a + 1.0)) * (i * u)

    for q in range(width // LRU_STRIP):
        steps.append((525, functools.partial(strip_pre, q), functools.partial(strip_mm, q),
                      functools.partial(strip_post, q)))

    def scan(r0):
        sub = lax.broadcasted_iota(jnp.int32, (SUBLANES, width), 0)
        tile_rows = lambda j: slice(r0 + j * SUBLANES, r0 + (j + 1) * SUBLANES)
        h = p = None
        for j in range(CHUNK_BLOCKS):
            a = a_ref[tile_rows(j), :]
            b = u_ref[tile_rows(j), :]
            h = b if j == 0 else a * h + b
            p = a if j == 0 else a * p
            u_ref[tile_rows(j), :] = h
            a_ref[tile_rows(j), :] = p
        for k in (1, 2, 4):
            keep = sub >= k
            p_sh = jnp.where(keep, pltpu.roll(p, k, 0), 1.0)
            h_sh = jnp.where(keep, pltpu.roll(h, k, 0), 0.0)
            h = p * h_sh + h
            p = p * p_sh
        h0 = hc_ref[...]
        ends = h + p * h0
        carry = jnp.where(sub == 0, h0, pltpu.roll(ends, 1, 0))
        hc_ref[...] = jnp.broadcast_to(ends[SUBLANES - 1:SUBLANES, :], (SUBLANES, width))
        for j in range(CHUNK_BLOCKS):
            h_true = u_ref[tile_rows(j), :] + a_ref[tile_rows(j), :] * carry
            gate = ld(tile_rows(j), slice(P_LRUG, P_LRUG + width))
            out_ref[tile_rows(j), :] = (jax.nn.gelu(gate) * h_true).astype(BF16)

    for r0 in range(0, t_len, SSD_L):
        steps.append((500, None, None, functools.partial(scan, r0)))
    return steps


def _split3(v):
    hi = v.astype(BF16)
    r1 = v - hi.astype(F32)
    mid = r1.astype(BF16)
    lo = (r1 - mid.astype(F32)).astype(BF16)
    return hi, mid, lo


def _ssd_steps(ld, ld_dt_t, ld_tail, r0, cw_ref, cb_ref, dtb_ref, dtb_t_ref,
               alog_ref, alog_t_ref, dskip_ref, ng_ref, out_ref, st_ref):
    l_len = SSD_L
    rs = slice(r0, r0 + l_len)
    ctx = {}

    def conv_silu(c0, width):
        cs = slice(c0, c0 + width)
        x = ld(rs, slice(P_XBC + c0, P_XBC + c0 + width))
        return jax.nn.silu(_causal_conv(x, ld_tail(cs), cw_ref, cb_ref, cs))

    def prologue_pre():
        def time_of(axis):
            r = lax.broadcasted_iota(jnp.int32, (l_len, l_len), axis)
            return (r % SUBLANES) * CHUNK_BLOCKS + r // SUBLANES

        rows = time_of(0)
        cols = time_of(1)
        causal = rows >= cols
        dt = jax.nn.softplus(ld(rs, slice(P_DT, P_DT + DT_PAD)) + dtb_ref[...])
        dt_t = jax.nn.softplus(ld_dt_t() + dtb_t_ref[...])
        lane = lax.broadcasted_iota(jnp.int32, (l_len, LANES), 1)
        ctx.update(
            causal=causal, dt=dt, first_head=lane < SSD_HEAD_DIM,
            lower=jnp.where(causal, 1.0, 0.0).astype(BF16),
            upper=jnp.where(rows <= cols, 1.0, 0.0).astype(BF16),
            a_parts=_split3(dt * -jnp.exp(alog_ref[...])),
            at_parts=_split3(dt_t * -jnp.exp(alog_t_ref[...])),
            b_all=conv_silu(SSD_INNER, SSD_BC).astype(BF16),
            c_all=conv_silu(SSD_INNER + SSD_BC, SSD_BC).astype(BF16))

    def scores_mm(g):
        b_g = ctx["b_all"][:, g * SSD_STATE:(g + 1) * SSD_STATE]
        c_g = ctx["c_all"][:, g * SSD_STATE:(g + 1) * SSD_STATE]
        scores = lax.dot_general(c_g, b_g, (((1,), (1,)), ((), ())),
                                 preferred_element_type=F32)
        ctx[g] = dict(b_g=b_g, c_g=c_g, scores=scores)

    def prologue_mm():
        lower, upper = ctx.pop("lower"), ctx.pop("upper")
        ctx["a_cs"] = sum(jnp.dot(lower, p, preferred_element_type=F32)
                          for p in reversed(ctx.pop("a_parts")))
        ctx["a_cs_t"] = sum(jnp.dot(p, upper, preferred_element_type=F32)
                            for p in reversed(ctx.pop("at_parts")))
        scores_mm(0)

    def expand(q, head0):
        tiles = []
        for j in range(PAIRS_PER_GROUP):
            h_a = head0 + 2 * j
            tiles.append(jnp.where(ctx["first_head"], q[:, h_a:h_a + 1], q[:, h_a + 1:h_a + 2]))
        return jnp.concatenate(tiles, axis=1)

    def group_pre(g):
        causal, a_cs, a_cs_t, first_head = (
            ctx["causal"], ctx["a_cs"], ctx["a_cs_t"], ctx["first_head"])
        c = ctx[g]
        gs = slice(g * GROUP_W, (g + 1) * GROUP_W)
        head0 = g * SSD_HEADS_PER_GROUP
        x_g = conv_silu(g * GROUP_W, GROUP_W)
        acs_x = expand(a_cs, head0)
        xdt = x_g * expand(ctx["dt"], head0)
        xdt_b = xdt.astype(BF16)
        lhs, rhs = [], []
        for j in range(PAIRS_PER_GROUP):
            x_pair = xdt_b[:, j * LANES:(j + 1) * LANES]
            m_pair, x_halves = [], []
            for half in range(2):
                h = head0 + 2 * j + half
                seg = a_cs[:, h:h + 1] - a_cs_t[h:h + 1, :]
                decay = jnp.exp(jnp.where(causal, seg, -jnp.inf))
                m_pair.append((c["scores"] * decay).astype(BF16))
                keep = first_head if half == 0 else jnp.logical_not(first_head)
                x_halves.append(jnp.where(keep, x_pair, jnp.zeros_like(x_pair)))
            lhs.append(jnp.concatenate(m_pair, axis=1))
            rhs.append(jnp.concatenate(x_halves, axis=0))
        last = acs_x[l_len - 1:l_len, :]
        st = st_ref[:, gs]
        c.update(x_g=x_g, lhs=lhs, rhs=rhs, e_cs=jnp.exp(acs_x), st_b=st.astype(BF16),
                 st_decayed=st * jnp.exp(last),
                 x_end=(xdt * jnp.exp(last - acs_x)).astype(BF16))

    def group_mm(g):
        c = ctx[g]
        lhs, rhs = c.pop("lhs"), c.pop("rhs")
        c["y_diag"] = jnp.concatenate(
            [jnp.dot(a, b, preferred_element_type=F32) for a, b in zip(lhs, rhs)], axis=1)
        c["y_off"] = jnp.dot(c["c_g"], c.pop("st_b"), preferred_element_type=F32)
        c["st_new"] = lax.dot_general(c["b_g"], c.pop("x_end"), (((0,), (0,)), ((), ())),
                                      preferred_element_type=F32)
        if g + 1 < SSD_GROUPS:
            scores_mm(g + 1)

    def group_post(g):
        c = ctx.pop(g)
        gs = slice(g * GROUP_W, (g + 1) * GROUP_W)
        st_ref[:, gs] = c["st_decayed"] + c["st_new"]
        y = c["y_diag"] + c["y_off"] * c["e_cs"] + dskip_ref[:, gs] * c["x_g"]
        y = y * jax.nn.silu(ld(rs, slice(P_Z + gs.start, P_Z + gs.stop)))
        y = y * lax.rsqrt(jnp.mean(y * y, axis=-1, keepdims=True) + EPS)
        out_ref[rs, gs] = (y * ng_ref[:, gs]).astype(BF16)

    steps = [(600, prologue_pre, prologue_mm, None)]
    for g in range(SSD_GROUPS):
        steps.append((950, functools.partial(group_pre, g), functools.partial(group_mm, g),
                      functools.partial(group_post, g)))
    return steps


def _mixer_in_kernel(h_ref, g_ref, w_ref, wdt_t_ref,
                     lcw_ref, lcb_ref, wa_ref, ba_ref, wx_ref, bx_ref, lam_ref,
                     scw_ref, scb_ref, dtb_ref, dtb_t_ref, alog_ref, alog_t_ref, dskip_ref, ng_ref,
                     ya_ref, yb_ref, gates_ref,
                     p_ref, dtt_ref, ltail_ref, a_ref, u_ref, hc_ref, stail_ref, st_ref,
                     *, tiles_per_seq):
    i = pl.program_id(0)
    slot = lax.rem(i, 2)
    prev = 1 - slot

    @pl.when(i == 0)
    def _():
        p_ref[1] = jnp.zeros(p_ref.shape[1:], F32)
        dtt_ref[1] = jnp.zeros(dtt_ref.shape[1:], F32)

    @pl.when(jnp.logical_or(i == 0, lax.rem(i + tiles_per_seq - 1, tiles_per_seq) == 0))
    def _():
        ltail_ref[...] = jnp.zeros_like(ltail_ref)
        hc_ref[...] = jnp.zeros_like(hc_ref)
        stail_ref[...] = jnp.zeros_like(stail_ref)
        st_ref[...] = jnp.zeros_like(st_ref)

    xb = _rms(h_ref[...], g_ref[...]).astype(BF16)

    def proj_cols(c, stop):
        p_ref[slot, :, c:stop] = jnp.dot(xb, w_ref[:, c:stop], preferred_element_type=F32)

    def proj_gates(c):
        gates_ref[:, c:c + PROJ_CHUNK] = jnp.dot(
            xb, w_ref[:, W_GATES + c:W_GATES + c + PROJ_CHUNK], preferred_element_type=F32)

    def proj_dt_t():
        dtt_ref[slot] = lax.dot_general(wdt_t_ref[...], xb, (((1,), (1,)), ((), ())),
                                        preferred_element_type=F32)

    proj_steps = [(min(c + PROJ_CHUNK, P_COLS) - c,
                   functools.partial(proj_cols, c, min(c + PROJ_CHUNK, P_COLS)))
                  for c in range(0, P_COLS, PROJ_CHUNK)]
    proj_steps += [(PROJ_CHUNK, functools.partial(proj_gates, c))
                   for c in range(0, 2 * D_MODEL, PROJ_CHUNK)]
    proj_steps.append((SSD_HEADS, proj_dt_t))

    def ld(rows, cols):
        return p_ref[prev, rows, cols]

    mixer_steps = _lru_steps(ld, lcw_ref, lcb_ref, wa_ref, ba_ref, wx_ref, bx_ref, lam_ref,
                             ya_ref, ltail_ref, a_ref, u_ref, hc_ref)
    for r0 in range(0, MIX_T, SSD_L):
        if r0 == 0:
            ld_tail = lambda cs: stail_ref[:, cs]
        else:
            ld_tail = lambda cs, r0=r0: ld(slice(r0 - TAIL_ROWS, r0),
                                           slice(P_XBC + cs.start, P_XBC + cs.stop))
        mixer_steps += _ssd_steps(
            ld, lambda r0=r0: dtt_ref[prev, :, r0:r0 + SSD_L], ld_tail, r0, scw_ref, scb_ref,
            dtb_ref, dtb_t_ref, alog_ref, alog_t_ref, dskip_ref, ng_ref, yb_ref, st_ref)

    def save_tail():
        stail_ref[...] = ld(slice(MIX_T - TAIL_ROWS, MIX_T), slice(P_XBC, P_XBC + SSD_CONV_DIM))

    mixer_steps.append((1, None, None, save_tail))
    _run_interleaved(mixer_steps, proj_steps)


def _mixer_in(h, g, w, wdt_t, lru_params, ssd_params, seq, layer):
    m = h.shape[0]
    n = m // MIX_T
    cur = lambda width: pl.BlockSpec((MIX_T, width), lambda i: (jnp.minimum(i, n - 1), 0))
    lag = lambda width: pl.BlockSpec((MIX_T, width), lambda i: (jnp.maximum(i - 1, 0), 0))
    n_strip = LRU_WIDTH // LRU_STRIP
    lru_specs = [_const_spec((CONV_W, LRU_WIDTH)), _const_spec((1, LRU_WIDTH)),
                 _const_spec((n_strip, LRU_STRIP, LRU_STRIP)), _const_spec((1, LRU_WIDTH)),
                 _const_spec((n_strip, LRU_STRIP, LRU_STRIP)), _const_spec((1, LRU_WIDTH)),
                 _const_spec((1, LRU_WIDTH))]
    ssd_specs = [_const_spec((CONV_W, SSD_CONV_DIM)), _const_spec((1, SSD_CONV_DIM)),
                 _const_spec((1, DT_PAD)), _const_spec((SSD_HEADS, 1)),
                 _const_spec((1, DT_PAD)), _const_spec((SSD_HEADS, 1)),
                 _const_spec((1, SSD_INNER)), _const_spec((1, SSD_INNER))]
    return pl.pallas_call(
        functools.partial(_mixer_in_kernel, tiles_per_seq=seq // MIX_T),
        grid=(n + 1,),
        in_specs=[cur(D_MODEL), _const_spec((1, D_MODEL)),
                  _layer_spec((D_MODEL, W_COLS), layer),
                  _layer_spec((SSD_HEADS, D_MODEL), layer)] + lru_specs + ssd_specs,
        out_specs=[lag(LRU_WIDTH), lag(SSD_INNER), cur(2 * D_MODEL)],
        out_shape=[jax.ShapeDtypeStruct((m, LRU_WIDTH), BF16),
                   jax.ShapeDtypeStruct((m, SSD_INNER), BF16),
                   jax.ShapeDtypeStruct((m, 2 * D_MODEL), F32)],
        scratch_shapes=[pltpu.VMEM((2, MIX_T, P_COLS), F32),
                        pltpu.VMEM((2, SSD_HEADS, MIX_T), F32),
                        pltpu.VMEM((TAIL_ROWS, LRU_WIDTH), F32),
                        pltpu.VMEM((MIX_T, LRU_WIDTH), F32),
                        pltpu.VMEM((MIX_T, LRU_WIDTH), F32),
                        pltpu.VMEM((SUBLANES, LRU_WIDTH), F32),
                        pltpu.VMEM((TAIL_ROWS, SSD_CONV_DIM), F32),
                        pltpu.VMEM((SSD_STATE, SSD_INNER), F32)],
        compiler_params=pltpu.CompilerParams(
            dimension_semantics=("arbitrary",), vmem_limit_bytes=VMEM_LIMIT_BYTES),
        name="mixer_in",
    )(h, g, w, wdt_t, *lru_params, *ssd_params)


def _merge_ffn_kernel(h_ref, ya_ref, yb_ref, gates_ref, bg_ref, wbr_ref, wout_ref,
                      g2_ref, w1_ref, w2_ref, gf_ref, out_ref, *, final_norm):
    g = jax.nn.sigmoid(gates_ref[...] + bg_ref[...])
    p_a = jnp.dot(ya_ref[...], wbr_ref[0:LRU_WIDTH, :], preferred_element_type=F32)
    p_b = jnp.dot(yb_ref[...], wbr_ref[LRU_WIDTH:, :], preferred_element_type=F32)
    merged = g[:, :D_MODEL] * p_a + g[:, D_MODEL:] * p_b
    h = h_ref[...] + jnp.dot(merged.astype(BF16), wout_ref[...], preferred_element_type=F32)

    xb = _rms(h, g2_ref[...]).astype(BF16)
    acc = h
    for c in range(0, D_FF, FFN_CHUNK):
        gate = jnp.dot(xb, w1_ref[:, c:c + FFN_CHUNK], preferred_element_type=F32)
        up = jnp.dot(xb, w1_ref[:, D_FF + c:D_FF + c + FFN_CHUNK], preferred_element_type=F32)
        act = (jax.nn.silu(gate) * up).astype(BF16)
        acc = acc + jnp.dot(act, w2_ref[c:c + FFN_CHUNK, :], preferred_element_type=F32)
    if final_norm:
        acc = _rms(acc, gf_ref[...])
    out_ref[...] = acc


def _merge_ffn(h, y_a, y_b, gates, bg, wbr, wout, g2, w1, w2, gf, final_norm, layer):
    m = h.shape[0]
    row = lambda width: pl.BlockSpec((FFN_TM, width), lambda i: (i, 0))
    return pl.pallas_call(
        functools.partial(_merge_ffn_kernel, final_norm=final_norm),
        grid=(m // FFN_TM,),
        in_specs=[row(D_MODEL), row(LRU_WIDTH), row(SSD_INNER), row(2 * D_MODEL),
                  _const_spec((1, 2 * D_MODEL)),
                  _layer_spec((LRU_WIDTH + SSD_INNER, D_MODEL), layer),
                  _layer_spec((D_MODEL, D_MODEL), layer),
                  _const_spec((1, D_MODEL)), _layer_spec((D_MODEL, 2 * D_FF), layer),
                  _layer_spec((D_FF, D_MODEL), layer), _const_spec((1, D_MODEL))],
        out_specs=row(D_MODEL),
        out_shape=jax.ShapeDtypeStruct((m, D_MODEL), F32),
        compiler_params=pltpu.CompilerParams(
            dimension_semantics=("arbitrary",), vmem_limit_bytes=VMEM_LIMIT_BYTES),
        name="merge_ffn",
    )(h, y_a, y_b, gates, bg, wbr, wout, g2, w1, w2, gf)


def _block_diag_tiles(w):
    per = LRU_STRIP // LRU_BLOCK_DIM
    w4 = w.reshape(LRU_BLOCKS // per, per, LRU_BLOCK_DIM, LRU_BLOCK_DIM)
    eye = jnp.eye(per, dtype=w.dtype)
    return jnp.einsum("qaij,ab->qaibj", w4, eye).reshape(
        LRU_BLOCKS // per, LRU_STRIP, LRU_STRIP).astype(BF16)


def _pad_lanes(v, width):
    return jnp.pad(v, ((0, 0), (0, width - v.shape[-1])))


def kernel(x, norm1_g, w_in, b_gate, lru_conv_w, lru_conv_b, lru_w_a, lru_b_a, lru_w_x, lru_b_x,
           lru_lambda, ssd_conv_w, ssd_conv_b, ssd_dt_bias, ssd_A_log, ssd_D, ssd_norm_g,
           w_branch, w_out, norm2_g, w_ffn_in, w_ffn_out, norm_f):
    batch, seq, d = x.shape
    depth = w_in.shape[0]
    m = batch * seq
    assert d == D_MODEL and seq % MIX_T == 0 and MIX_T % SSD_L == 0 and m % FFN_TM == 0

    o_dt = 2 * LRU_WIDTH + SSD_INNER + SSD_CONV_DIM
    o_g = o_dt + SSD_HEADS
    row2 = lambda v: v.reshape(1, -1)

    w_dt = w_in[:, :, o_dt:o_g]
    w_in_b = jnp.concatenate(
        [w_in[:, :, :o_dt].astype(BF16),
         jnp.pad(w_dt.astype(BF16), ((0, 0), (0, 0), (0, DT_PAD - SSD_HEADS))),
         w_in[:, :, o_g:].astype(BF16)], axis=2)
    w_dt_t_b = jnp.swapaxes(w_dt, 1, 2).astype(BF16)
    w_branch_b, w_out_b = w_branch.astype(BF16), w_out.astype(BF16)
    w_ffn_in_b, w_ffn_out_b = w_ffn_in.astype(BF16), w_ffn_out.astype(BF16)

    h = x.reshape(m // SSD_L, SUBLANES, CHUNK_BLOCKS, d).swapaxes(1, 2).reshape(m, d)
    for l in range(depth):
        lru_params = (lru_conv_w[l], row2(lru_conv_b[l]),
                      _block_diag_tiles(lru_w_a[l]), row2(lru_b_a[l]),
                      _block_diag_tiles(lru_w_x[l]), row2(lru_b_x[l]), row2(lru_lambda[l]))
        ssd_params = (ssd_conv_w[l], row2(ssd_conv_b[l]),
                      _pad_lanes(row2(ssd_dt_bias[l]), DT_PAD), ssd_dt_bias[l].reshape(-1, 1),
                      _pad_lanes(row2(ssd_A_log[l]), DT_PAD), ssd_A_log[l].reshape(-1, 1),
                      row2(jnp.repeat(ssd_D[l], SSD_HEAD_DIM)), row2(ssd_norm_g[l]))
        y_a, y_b, gates = _mixer_in(h, row2(norm1_g[l]), w_in_b, w_dt_t_b,
                                    lru_params, ssd_params, seq, l)
        h = _merge_ffn(h, y_a, y_b, gates, row2(b_gate[l]), w_branch_b, w_out_b,
                       row2(norm2_g[l]), w_ffn_in_b, w_ffn_out_b,
                       row2(norm_f), final_norm=(l == depth - 1), layer=l)
    h = h.reshape(m // SSD_L, CHUNK_BLOCKS, SUBLANES, d).swapaxes(1, 2)
    return h.reshape(batch, seq, d)
```

```python
import functools

import jax
import jax.numpy as jnp
from jax import lax
from jax.experimental import pallas as pl
from jax.experimental.pallas import tpu as pltpu

F32 = jnp.float32
BF16 = jnp.bfloat16

D_MODEL = 1024
CONV_W = 4
LRU_WIDTH = D_MODEL
LRU_BLOCKS = 16
LRU_BLOCK_DIM = LRU_WIDTH // LRU_BLOCKS
LRU_C = 8.0
SSD_INNER = 2 * D_MODEL
SSD_HEAD_DIM = 64
SSD_HEADS = SSD_INNER // SSD_HEAD_DIM
SSD_GROUPS = 4
SSD_HEADS_PER_GROUP = SSD_HEADS // SSD_GROUPS
SSD_STATE = 128
SSD_BC = SSD_GROUPS * SSD_STATE
SSD_CONV_DIM = SSD_INNER + 2 * SSD_BC
D_FF = 2816
EPS = 1e-6

LANES = 128
SUBLANES = 8
MXU_DIM = 256
VMEM_LIMIT_BYTES = 56 * 1024 * 1024

GROUP_W = SSD_INNER // SSD_GROUPS
PAIRS_PER_GROUP = GROUP_W // LANES
DT_PAD = LANES

MIX_T = 256
SSD_L = 128
FFN_TM = 512
FFN_CHUNK = D_FF // 2
PROJ_CHUNK = 256
LRU_STRIP = MXU_DIM

CHUNK_BLOCKS = SSD_L // SUBLANES
TAIL_ROWS = (CONV_W - 1) * SUBLANES

P_LRUX = 0
P_LRUG = P_LRUX + LRU_WIDTH
P_Z = P_LRUG + LRU_WIDTH
P_XBC = P_Z + SSD_INNER
P_DT = P_XBC + SSD_CONV_DIM
P_COLS = P_DT + DT_PAD
W_GATES = P_COLS
W_COLS = W_GATES + 2 * D_MODEL


def _const_spec(shape):
    nd = len(shape)
    return pl.BlockSpec(shape, lambda *_: (0,) * nd, pipeline_mode=pl.Buffered(1))


def _layer_spec(shape, layer):
    nd = len(shape)
    return pl.BlockSpec((None,) + shape, lambda *_: (layer,) + (0,) * nd,
                        pipeline_mode=pl.Buffered(1))


def _rms(x, g):
    return x * lax.rsqrt(jnp.mean(x * x, axis=-1, keepdims=True) + EPS) * g


def _causal_conv(x, tail, cw_ref, cb_ref, cs):
    sub = lax.broadcasted_iota(jnp.int32, (SUBLANES, x.shape[1]), 0)
    tiles = [x[j * SUBLANES:(j + 1) * SUBLANES, :] for j in range(CHUNK_BLOCKS)]
    wrapped = []
    for m in range(CONV_W - 1):
        cur = pltpu.roll(tiles[CHUNK_BLOCKS - (CONV_W - 1) + m], 1, 0)
        prv = pltpu.roll(tail[m * SUBLANES:(m + 1) * SUBLANES, :], 1, 0)
        wrapped.append(jnp.where(sub == 0, prv, cur))
    ext = wrapped + tiles
    acc = cb_ref[:, cs] + cw_ref[CONV_W - 1:CONV_W, cs] * x
    for k in range(1, CONV_W):
        shifted = jnp.concatenate(ext[CONV_W - 1 - k:CONV_W - 1 - k + CHUNK_BLOCKS], axis=0)
        acc = acc + cw_ref[CONV_W - 1 - k:CONV_W - k, cs] * shifted
    return acc


def _run_interleaved(main_steps, fill_steps):
    total_main = sum(s[0] for s in main_steps)
    total_fill = sum(w for w, _ in fill_steps)
    done_main = 0.0
    done_fill = 0.0
    fill = list(fill_steps)
    pending_post = None
    for w, pre, mm, post in main_steps:
        if pre is not None:
            pre()
        if pending_post is not None:
            pending_post()
        done_main += w
        while fill and done_fill / total_fill < done_main / total_main:
            fw, fstep = fill.pop(0)
            fstep()
            done_fill += fw
        if mm is not None:
            mm()
        pending_post = post
    if pending_post is not None:
        pending_post()
    for _, fstep in fill:
        fstep()


def _lru_steps(ld, cw_ref, cb_ref, wa_ref, ba_ref, wx_ref, bx_ref, lam_ref,
               out_ref, tail_ref, a_ref, u_ref, hc_ref):
    t_len, width = a_ref.shape
    steps = []

    ctx = {}

    def strip_pre(q):
        cs = slice(q * LRU_STRIP, (q + 1) * LRU_STRIP)
        pcs = slice(P_LRUX + cs.start, P_LRUX + cs.stop)
        tail = tail_ref[:, cs]
        convs = []
        for r0 in range(0, t_len, SSD_L):
            x = ld(slice(r0, r0 + SSD_L), pcs)
            convs.append(_causal_conv(x, tail, cw_ref, cb_ref, cs))
            tail = x[SSD_L - TAIL_ROWS:SSD_L, :]
        tail_ref[:, cs] = tail
        u = jnp.concatenate(convs, axis=0)
        ctx[q] = (u, u.astype(BF16))

    def strip_mm(q):
        u, ub = ctx[q]
        ctx[q] = (u, jnp.dot(ub, wa_ref[q], preferred_element_type=F32),
                  jnp.dot(ub, wx_ref[q], preferred_element_type=F32))

    def strip_post(q):
        cs = slice(q * LRU_STRIP, (q + 1) * LRU_STRIP)
        u, r_lin, i_lin = ctx.pop(q)
        r = jax.nn.sigmoid(r_lin + ba_ref[:, cs])
        i = jax.nn.sigmoid(i_lin + bx_ref[:, cs])
        log_a = -LRU_C * r * jax.nn.softplus(-lam_ref[:, cs])
        a = jnp.exp(log_a)
        a_ref[:, cs] = a
        u_ref[:, cs] = jnp.sqrt(-jnp.tanh(log_a) * (a * a + 1.0)) * (i * u)

    for q in range(width // LRU_STRIP):
        steps.append((525, functools.partial(strip_pre, q), functools.partial(strip_mm, q),
                      functools.partial(strip_post, q)))

    def scan(r0):
        sub = lax.broadcasted_iota(jnp.int32, (SUBLANES, width), 0)
        tile_rows = lambda j: slice(r0 + j * SUBLANES, r0 + (j + 1) * SUBLANES)
        h = p = None
        for j in range(CHUNK_BLOCKS):
            a = a_ref[tile_rows(j), :]
            b = u_ref[tile_rows(j), :]
            h = b if j == 0 else a * h + b
            p = a if j == 0 else a * p
            u_ref[tile_rows(j), :] = h
            a_ref[tile_rows(j), :] = p
        for k in (1, 2, 4):
            keep = sub >= k
            p_sh = jnp.where(keep, pltpu.roll(p, k, 0), 1.0)
            h_sh = jnp.where(keep, pltpu.roll(h, k, 0), 0.0)
            h = p * h_sh + h
            p = p * p_sh
        h0 = hc_ref[...]
        ends = h + p * h0
        carry = jnp.where(sub == 0, h0, pltpu.roll(ends, 1, 0))
        hc_ref[...] = jnp.broadcast_to(ends[SUBLANES - 1:SUBLANES, :], (SUBLANES, width))
        for j in range(CHUNK_BLOCKS):
            h_true = u_ref[tile_rows(j), :] + a_ref[tile_rows(j), :] * carry
            gate = ld(tile_rows(j), slice(P_LRUG, P_LRUG + width))
            out_ref[tile_rows(j), :] = (jax.nn.gelu(gate) * h_true).astype(BF16)

    for r0 in range(0, t_len, SSD_L):
        steps.append((500, None, None, functools.partial(scan, r0)))
    return steps


def _split3(v):
    hi = v.astype(BF16)
    r1 = v - hi.astype(F32)
    mid = r1.astype(BF16)
    lo = (r1 - mid.astype(F32)).astype(BF16)
    return hi, mid, lo


def _ssd_steps(ld, ld_dt_t, ld_tail, r0, cw_ref, cb_ref, dtb_ref, dtb_t_ref,
               alog_ref, alog_t_ref, dskip_ref, ng_ref, out_ref, st_ref):
    l_len = SSD_L
    rs = slice(r0, r0 + l_len)
    ctx = {}

    def conv_silu(c0, width):
        cs = slice(c0, c0 + width)
        x = ld(rs, slice(P_XBC + c0, P_XBC + c0 + width))
        return jax.nn.silu(_causal_conv(x, ld_tail(cs), cw_ref, cb_ref, cs))

    def prologue_pre():
        def time_of(axis):
            r = lax.broadcasted_iota(jnp.int32, (l_len, l_len), axis)
            return (r % SUBLANES) * CHUNK_BLOCKS + r // SUBLANES

        rows = time_of(0)
        cols = time_of(1)
        causal = rows >= cols
        dt = jax.nn.softplus(ld(rs, slice(P_DT, P_DT + DT_PAD)) + dtb_ref[...])
        dt_t = jax.nn.softplus(ld_dt_t() + dtb_t_ref[...])
        lane = lax.broadcasted_iota(jnp.int32, (l_len, LANES), 1)
        ctx.update(
            causal=causal, dt=dt, first_head=lane < SSD_HEAD_DIM,
            lower=jnp.where(causal, 1.0, 0.0).astype(BF16),
            upper=jnp.where(rows <= cols, 1.0, 0.0).astype(BF16),
            a_parts=_split3(dt * -jnp.exp(alog_ref[...])),
            at_parts=_split3(dt_t * -jnp.exp(alog_t_ref[...])),
            b_all=conv_silu(SSD_INNER, SSD_BC).astype(BF16),
            c_all=conv_silu(SSD_INNER + SSD_BC, SSD_BC).astype(BF16))

    def scores_mm(g):
        b_g = ctx["b_all"][:, g * SSD_STATE:(g + 1) * SSD_STATE]
        c_g = ctx["c_all"][:, g * SSD_STATE:(g + 1) * SSD_STATE]
        scores = lax.dot_general(c_g, b_g, (((1,), (1,)), ((), ())),
                                 preferred_element_type=F32)
        ctx[g] = dict(b_g=b_g, c_g=c_g, scores=scores)

    def prologue_mm():
        lower, upper = ctx.pop("lower"), ctx.pop("upper")
        ctx["a_cs"] = sum(jnp.dot(lower, p, preferred_element_type=F32)
                          for p in reversed(ctx.pop("a_parts")))
        ctx["a_cs_t"] = sum(jnp.dot(p, upper, preferred_element_type=F32)
                            for p in reversed(ctx.pop("at_parts")))
        scores_mm(0)

    def expand(q, head0):
        tiles = []
        for j in range(PAIRS_PER_GROUP):
            h_a = head0 + 2 * j
            tiles.append(jnp.where(ctx["first_head"], q[:, h_a:h_a + 1], q[:, h_a + 1:h_a + 2]))
        return jnp.concatenate(tiles, axis=1)

    def group_pre(g):
        causal, a_cs, a_cs_t, first_head = (
            ctx["causal"], ctx["a_cs"], ctx["a_cs_t"], ctx["first_head"])
        c = ctx[g]
        gs = slice(g * GROUP_W, (g + 1) * GROUP_W)
        head0 = g * SSD_HEADS_PER_GROUP
        x_g = conv_silu(g * GROUP_W, GROUP_W)
        acs_x = expand(a_cs, head0)
        xdt = x_g * expand(ctx["dt"], head0)
        xdt_b = xdt.astype(BF16)
        lhs, rhs = [], []
        for j in range(PAIRS_PER_GROUP):
            x_pair = xdt_b[:, j * LANES:(j + 1) * LANES]
            m_pair, x_halves = [], []
            for half in range(2):
                h = head0 + 2 * j + half
                seg = a_cs[:, h:h + 1] - a_cs_t[h:h + 1, :]
                decay = jnp.exp(jnp.where(causal, seg, -jnp.inf))
                m_pair.append((c["scores"] * decay).astype(BF16))
                keep = first_head if half == 0 else jnp.logical_not(first_head)
                x_halves.append(jnp.where(keep, x_pair, jnp.zeros_like(x_pair)))
            lhs.append(jnp.concatenate(m_pair, axis=1))
            rhs.append(jnp.concatenate(x_halves, axis=0))
        last = acs_x[l_len - 1:l_len, :]
        st = st_ref[:, gs]
        c.update(x_g=x_g, lhs=lhs, rhs=rhs, e_cs=jnp.exp(acs_x), st_b=st.astype(BF16),
                 st_decayed=st * jnp.exp(last),
                 x_end=(xdt * jnp.exp(last - acs_x)).astype(BF16))

    def group_mm(g):
        c = ctx[g]
        lhs, rhs = c.pop("lhs"), c.pop("rhs")
        c["y_diag"] = jnp.concatenate(
            [jnp.dot(a, b, preferred_element_type=F32) for a, b in zip(lhs, rhs)], axis=1)
        c["y_off"] = jnp.dot(c["c_g"], c.pop("st_b"), preferred_element_type=F32)
        c["st_new"] = lax.dot_general(c["b_g"], c.pop("x_end"), (((0,), (0,)), ((), ())),
                                      preferred_element_type=F32)
        if g + 1 < SSD_GROUPS:
            scores_mm(g + 1)

    def group_post(g):
        c = ctx.pop(g)
        gs = slice(g * GROUP_W, (g + 1) * GROUP_W)
        st_ref[:, gs] = c["st_decayed"] + c["st_new"]
        y = c["y_diag"] + c["y_off"] * c["e_cs"] + dskip_ref[:, gs] * c["x_g"]
        y = y * jax.nn.silu(ld(rs, slice(P_Z + gs.start, P_Z + gs.stop)))
        y = y * lax.rsqrt(jnp.mean(y * y, axis=-1, keepdims=True) + EPS)
        out_ref[rs, gs] = (y * ng_ref[:, gs]).astype(BF16)

    steps = [(600, prologue_pre, prologue_mm, None)]
    for g in range(SSD_GROUPS):
        steps.append((950, functools.partial(group_pre, g), functools.partial(group_mm, g),
                      functools.partial(group_post, g)))
    return steps


def _mixer_in_kernel(h_ref, g_ref, w_ref, wdt_t_ref,
                     lcw_ref, lcb_ref, wa_ref, ba_ref, wx_ref, bx_ref, lam_ref,
                     scw_ref, scb_ref, dtb_ref, dtb_t_ref, alog_ref, alog_t_ref, dskip_ref, ng_ref,
                     ya_ref, yb_ref, gates_ref,
                     p_ref, dtt_ref, ltail_ref, a_ref, u_ref, hc_ref, stail_ref, st_ref,
                     *, tiles_per_seq):
    i = pl.program_id(0)
    slot = lax.rem(i, 2)
    prev = 1 - slot

    @pl.when(i == 0)
    def _():
        p_ref[1] = jnp.zeros(p_ref.shape[1:], F32)
        dtt_ref[1] = jnp.zeros(dtt_ref.shape[1:], F32)

    @pl.when(jnp.logical_or(i == 0, lax.rem(i + tiles_per_seq - 1, tiles_per_seq) == 0))
    def _():
        ltail_ref[...] = jnp.zeros_like(ltail_ref)
        hc_ref[...] = jnp.zeros_like(hc_ref)
        stail_ref[...] = jnp.zeros_like(stail_ref)
        st_ref[...] = jnp.zeros_like(st_ref)

    xb = _rms(h_ref[...], g_ref[...]).astype(BF16)

    def proj_cols(c, stop):
        p_ref[slot, :, c:stop] = jnp.dot(xb, w_ref[:, c:stop], preferred_element_type=F32)

    def proj_gates(c):
        gates_ref[:, c:c + PROJ_CHUNK] = jnp.dot(
            xb, w_ref[:, W_GATES + c:W_GATES + c + PROJ_CHUNK], preferred_element_type=F32)

    def proj_dt_t():
        dtt_ref[slot] = lax.dot_general(wdt_t_ref[...], xb, (((1,), (1,)), ((), ())),
                                        preferred_element_type=F32)

    proj_steps = [(min(c + PROJ_CHUNK, P_COLS) - c,
                   functools.partial(proj_cols, c, min(c + PROJ_CHUNK, P_COLS)))
                  for c in range(0, P_COLS, PROJ_CHUNK)]
    proj_steps += [(PROJ_CHUNK, functools.partial(proj_gates, c))
                   for c in range(0, 2 * D_MODEL, PROJ_CHUNK)]
    proj_steps.append((SSD_HEADS, proj_dt_t))

    def ld(rows, cols):
        return p_ref[prev, rows, cols]

    mixer_steps = _lru_steps(ld, lcw_ref, lcb_ref, wa_ref, ba_ref, wx_ref, bx_ref, lam_ref,
                             ya_ref, ltail_ref, a_ref, u_ref, hc_ref)
    for r0 in range(0, MIX_T, SSD_L):
        if r0 == 0:
            ld_tail = lambda cs: stail_ref[:, cs]
        else:
            ld_tail = lambda cs, r0=r0: ld(slice(r0 - TAIL_ROWS, r0),
                                           slice(P_XBC + cs.start, P_XBC + cs.stop))
        mixer_steps += _ssd_steps(
            ld, lambda r0=r0: dtt_ref[prev, :, r0:r0 + SSD_L], ld_tail, r0, scw_ref, scb_ref,
            dtb_ref, dtb_t_ref, alog_ref, alog_t_ref, dskip_ref, ng_ref, yb_ref, st_ref)

    def save_tail():
        stail_ref[...] = ld(slice(MIX_T - TAIL_ROWS, MIX_T), slice(P_XBC, P_XBC + SSD_CONV_DIM))

    mixer_steps.append((1, None, None, save_tail))
    _run_interleaved(mixer_steps, proj_steps)


def _mixer_in(h, g, w, wdt_t, lru_params, ssd_params, seq, layer):
    m = h.shape[0]
    n = m // MIX_T
    cur = lambda width: pl.BlockSpec((MIX_T, width), lambda i: (jnp.minimum(i, n - 1), 0))
    lag = lambda width: pl.BlockSpec((MIX_T, width), lambda i: (jnp.maximum(i - 1, 0), 0))
    n_strip = LRU_WIDTH // LRU_STRIP
    lru_specs = [_const_spec((CONV_W, LRU_WIDTH)), _const_spec((1, LRU_WIDTH)),
                 _const_spec((n_strip, LRU_STRIP, LRU_STRIP)), _const_spec((1, LRU_WIDTH)),
                 _const_spec((n_strip, LRU_STRIP, LRU_STRIP)), _const_spec((1, LRU_WIDTH)),
                 _const_spec((1, LRU_WIDTH))]
    ssd_specs = [_const_spec((CONV_W, SSD_CONV_DIM)), _const_spec((1, SSD_CONV_DIM)),
                 _const_spec((1, DT_PAD)), _const_spec((SSD_HEADS, 1)),
                 _const_spec((1, DT_PAD)), _const_spec((SSD_HEADS, 1)),
                 _const_spec((1, SSD_INNER)), _const_spec((1, SSD_INNER))]
    return pl.pallas_call(
        functools.partial(_mixer_in_kernel, tiles_per_seq=seq // MIX_T),
        grid=(n + 1,),
        in_specs=[cur(D_MODEL), _const_spec((1, D_MODEL)),
                  _layer_spec((D_MODEL, W_COLS), layer),
                  _layer_spec((SSD_HEADS, D_MODEL), layer)] + lru_specs + ssd_specs,
        out_specs=[lag(LRU_WIDTH), lag(SSD_INNER), cur(2 * D_MODEL)],
        out_shape=[jax.ShapeDtypeStruct((m, LRU_WIDTH), BF16),
                   jax.ShapeDtypeStruct((m, SSD_INNER), BF16),
                   jax.ShapeDtypeStruct((m, 2 * D_MODEL), F32)],
        scratch_shapes=[pltpu.VMEM((2, MIX_T, P_COLS), F32),
                        pltpu.VMEM((2, SSD_HEADS, MIX_T), F32),
                        pltpu.VMEM((TAIL_ROWS, LRU_WIDTH), F32),
                        pltpu.VMEM((MIX_T, LRU_WIDTH), F32),
                        pltpu.VMEM((MIX_T, LRU_WIDTH), F32),
                        pltpu.VMEM((SUBLANES, LRU_WIDTH), F32),
                        pltpu.VMEM((TAIL_ROWS, SSD_CONV_DIM), F32),
                        pltpu.VMEM((SSD_STATE, SSD_INNER), F32)],
        compiler_params=pltpu.CompilerParams(
            dimension_semantics=("arbitrary",), vmem_limit_bytes=VMEM_LIMIT_BYTES),
        name="mixer_in",
    )(h, g, w, wdt_t, *lru_params, *ssd_params)


def _merge_ffn_kernel(h_ref, ya_ref, yb_ref, gates_ref, bg_ref, wbr_ref, wout_ref,
                      g2_ref, w1_ref, w2_ref, gf_ref, out_ref, *, final_norm):
    g = jax.nn.sigmoid(gates_ref[...] + bg_ref[...])
    p_a = jnp.dot(ya_ref[...], wbr_ref[0:LRU_WIDTH, :], preferred_element_type=F32)
    p_b = jnp.dot(yb_ref[...], wbr_ref[LRU_WIDTH:, :], preferred_element_type=F32)
    merged = g[:, :D_MODEL] * p_a + g[:, D_MODEL:] * p_b
    h = h_ref[...] + jnp.dot(merged.astype(BF16), wout_ref[...], preferred_element_type=F32)

    xb = _rms(h, g2_ref[...]).astype(BF16)
    acc = h
    for c in range(0, D_FF, FFN_CHUNK):
        gate = jnp.dot(xb, w1_ref[:, c:c + FFN_CHUNK], preferred_element_type=F32)
        up = jnp.dot(xb, w1_ref[:, D_FF + c:D_FF + c + FFN_CHUNK], preferred_element_type=F32)
        act = (jax.nn.silu(gate) * up).astype(BF16)
        acc = acc + jnp.dot(act, w2_ref[c:c + FFN_CHUNK, :], preferred_element_type=F32)
    if final_norm:
        acc = _rms(acc, gf_ref[...])
        for r0 in range(0, acc.shape[0], SSD_L):
            chunk = acc[r0:r0 + SSD_L, :].reshape(CHUNK_BLOCKS, SUBLANES, D_MODEL)
            out_ref[r0:r0 + SSD_L, :] = pltpu.einshape("jsd->sjd", chunk).reshape(SSD_L, D_MODEL)
    else:
        out_ref[...] = acc


def _merge_ffn(h, y_a, y_b, gates, bg, wbr, wout, g2, w1, w2, gf, final_norm, layer):
    m = h.shape[0]
    row = lambda width: pl.BlockSpec((FFN_TM, width), lambda i: (i, 0))
    return pl.pallas_call(
        functools.partial(_merge_ffn_kernel, final_norm=final_norm),
        grid=(m // FFN_TM,),
        in_specs=[row(D_MODEL), row(LRU_WIDTH), row(SSD_INNER), row(2 * D_MODEL),
                  _const_spec((1, 2 * D_MODEL)),
                  _layer_spec((LRU_WIDTH + SSD_INNER, D_MODEL), layer),
                  _layer_spec((D_MODEL, D_MODEL), layer),
                  _const_spec((1, D_MODEL)), _layer_spec((D_MODEL, 2 * D_FF), layer),
                  _layer_spec((D_FF, D_MODEL), layer), _const_spec((1, D_MODEL))],
        out_specs=row(D_MODEL),
        out_shape=jax.ShapeDtypeStruct((m, D_MODEL), F32),
        compiler_params=pltpu.CompilerParams(
            dimension_semantics=("arbitrary",), vmem_limit_bytes=VMEM_LIMIT_BYTES),
        name="merge_ffn",
    )(h, y_a, y_b, gates, bg, wbr, wout, g2, w1, w2, gf)


def _block_diag_tiles(w):
    per = LRU_STRIP // LRU_BLOCK_DIM
    w4 = w.reshape(LRU_BLOCKS // per, per, LRU_BLOCK_DIM, LRU_BLOCK_DIM)
    eye = jnp.eye(per, dtype=w.dtype)
    return jnp.einsum("qaij,ab->qaibj", w4, eye).reshape(
        LRU_BLOCKS // per, LRU_STRIP, LRU_STRIP).astype(BF16)


def _pad_lanes(v, width):
    return jnp.pad(v, ((0, 0), (0, width - v.shape[-1])))


def kernel(x, norm1_g, w_in, b_gate, lru_conv_w, lru_conv_b, lru_w_a, lru_b_a, lru_w_x, lru_b_x,
           lru_lambda, ssd_conv_w, ssd_conv_b, ssd_dt_bias, ssd_A_log, ssd_D, ssd_norm_g,
           w_branch, w_out, norm2_g, w_ffn_in, w_ffn_out, norm_f):
    batch, seq, d = x.shape
    depth = w_in.shape[0]
    m = batch * seq
    assert d == D_MODEL and seq % MIX_T == 0 and MIX_T % SSD_L == 0 and m % FFN_TM == 0

    o_dt = 2 * LRU_WIDTH + SSD_INNER + SSD_CONV_DIM
    o_g = o_dt + SSD_HEADS
    row2 = lambda v: v.reshape(1, -1)

    w_in_b = w_in.astype(BF16)
    w_dt_b = w_in_b[:, :, o_dt:o_g]
    w_dt_t_b = jnp.swapaxes(w_dt_b, 1, 2)
    w_in_b = jnp.concatenate(
        [w_in_b[:, :, :o_dt], jnp.pad(w_dt_b, ((0, 0), (0, 0), (0, DT_PAD - SSD_HEADS))),
         w_in_b[:, :, o_g:]], axis=2)
    w_branch_b, w_out_b = w_branch.astype(BF16), w_out.astype(BF16)
    w_ffn_in_b, w_ffn_out_b = w_ffn_in.astype(BF16), w_ffn_out.astype(BF16)

    h = x.reshape(m // SSD_L, SUBLANES, CHUNK_BLOCKS, d).swapaxes(1, 2).reshape(m, d)
    for l in range(depth):
        lru_params = (lru_conv_w[l], row2(lru_conv_b[l]),
                      _block_diag_tiles(lru_w_a[l]), row2(lru_b_a[l]),
                      _block_diag_tiles(lru_w_x[l]), row2(lru_b_x[l]), row2(lru_lambda[l]))
        ssd_params = (ssd_conv_w[l], row2(ssd_conv_b[l]),
                      _pad_lanes(row2(ssd_dt_bias[l]), DT_PAD), ssd_dt_bias[l].reshape(-1, 1),
                      _pad_lanes(row2(ssd_A_log[l]), DT_PAD), ssd_A_log[l].reshape(-1, 1),
                      row2(jnp.repeat(ssd_D[l], SSD_HEAD_DIM)), row2(ssd_norm_g[l]))
        y_a, y_b, gates = _mixer_in(h, row2(norm1_g[l]), w_in_b, w_dt_t_b,
                                    lru_params, ssd_params, seq, l)
        h = _merge_ffn(h, y_a, y_b, gates, row2(b_gate[l]), w_branch_b, w_out_b,
                       row2(norm2_g[l]), w_ffn_in_b, w_ffn_out_b,
                       row2(norm_f), final_norm=(l == depth - 1), layer=l)
    return h.reshape(batch, seq, d)
```

```python
import functools

import jax
import jax.numpy as jnp
from jax import lax
from jax.experimental import pallas as pl
from jax.experimental.pallas import tpu as pltpu

F32 = jnp.float32
BF16 = jnp.bfloat16

D_MODEL = 1024
CONV_W = 4
LRU_WIDTH = D_MODEL
LRU_BLOCKS = 16
LRU_BLOCK_DIM = LRU_WIDTH // LRU_BLOCKS
LRU_C = 8.0
SSD_INNER = 2 * D_MODEL
SSD_HEAD_DIM = 64
SSD_HEADS = SSD_INNER // SSD_HEAD_DIM
SSD_GROUPS = 4
SSD_HEADS_PER_GROUP = SSD_HEADS // SSD_GROUPS
SSD_STATE = 128
SSD_BC = SSD_GROUPS * SSD_STATE
SSD_CONV_DIM = SSD_INNER + 2 * SSD_BC
D_FF = 2816
EPS = 1e-6

LANES = 128
SUBLANES = 8
MXU_DIM = 256
VMEM_LIMIT_BYTES = 56 * 1024 * 1024

GROUP_W = SSD_INNER // SSD_GROUPS
PAIRS_PER_GROUP = GROUP_W // LANES
DT_PAD = LANES

MIX_T = 256
SSD_L = 128
FFN_TM = 512
FFN_CHUNK = D_FF // 2
PROJ_CHUNK = 256
LRU_STRIP = MXU_DIM

CHUNK_BLOCKS = SSD_L // SUBLANES
TAIL_ROWS = (CONV_W - 1) * SUBLANES

P_LRUX = 0
P_LRUG = P_LRUX + LRU_WIDTH
P_Z = P_LRUG + LRU_WIDTH
P_XBC = P_Z + SSD_INNER
P_DT = P_XBC + SSD_CONV_DIM
P_COLS = P_DT + DT_PAD
W_GATES = P_COLS
W_COLS = W_GATES + 2 * D_MODEL


def _const_spec(shape):
    nd = len(shape)
    return pl.BlockSpec(shape, lambda *_: (0,) * nd, pipeline_mode=pl.Buffered(1))


def _layer_spec(shape, layer):
    nd = len(shape)
    return pl.BlockSpec((None,) + shape, lambda *_: (layer,) + (0,) * nd,
                        pipeline_mode=pl.Buffered(1))


def _rms(x, g):
    return x * lax.rsqrt(jnp.mean(x * x, axis=-1, keepdims=True) + EPS) * g


def _to_permuted_rows(x):
    return jnp.concatenate(
        [pltpu.einshape("sjd->jsd", x[r0:r0 + SSD_L, :].reshape(
            SUBLANES, CHUNK_BLOCKS, x.shape[1])).reshape(SSD_L, x.shape[1])
         for r0 in range(0, x.shape[0], SSD_L)], axis=0)


def _causal_conv(x, tail, cw_ref, cb_ref, cs):
    sub = lax.broadcasted_iota(jnp.int32, (SUBLANES, x.shape[1]), 0)
    tiles = [x[j * SUBLANES:(j + 1) * SUBLANES, :] for j in range(CHUNK_BLOCKS)]
    wrapped = []
    for m in range(CONV_W - 1):
        cur = pltpu.roll(tiles[CHUNK_BLOCKS - (CONV_W - 1) + m], 1, 0)
        prv = pltpu.roll(tail[m * SUBLANES:(m + 1) * SUBLANES, :], 1, 0)
        wrapped.append(jnp.where(sub == 0, prv, cur))
    ext = wrapped + tiles
    acc = cb_ref[:, cs] + cw_ref[CONV_W - 1:CONV_W, cs] * x
    for k in range(1, CONV_W):
        shifted = jnp.concatenate(ext[CONV_W - 1 - k:CONV_W - 1 - k + CHUNK_BLOCKS], axis=0)
        acc = acc + cw_ref[CONV_W - 1 - k:CONV_W - k, cs] * shifted
    return acc


def _run_interleaved(main_steps, fill_steps):
    total_main = sum(s[0] for s in main_steps)
    total_fill = sum(w for w, _ in fill_steps)
    done_main = 0.0
    done_fill = 0.0
    fill = list(fill_steps)
    pending_post = None
    for w, pre, mm, post in main_steps:
        if pre is not None:
            pre()
        if pending_post is not None:
            pending_post()
        done_main += w
        while fill and done_fill / total_fill < done_main / total_main:
            fw, fstep = fill.pop(0)
            fstep()
            done_fill += fw
        if mm is not None:
            mm()
        pending_post = post
    if pending_post is not None:
        pending_post()
    for _, fstep in fill:
        fstep()


def _lru_steps(ld, cw_ref, cb_ref, wa_ref, ba_ref, wx_ref, bx_ref, lam_ref,
               out_ref, tail_ref, a_ref, u_ref, hc_ref):
    t_len, width = a_ref.shape
    steps = []

    ctx = {}

    def strip_pre(q):
        cs = slice(q * LRU_STRIP, (q + 1) * LRU_STRIP)
        pcs = slice(P_LRUX + cs.start, P_LRUX + cs.stop)
        tail = tail_ref[:, cs]
        convs = []
        for r0 in range(0, t_len, SSD_L):
            x = ld(slice(r0, r0 + SSD_L), pcs)
            convs.append(_causal_conv(x, tail, cw_ref, cb_ref, cs))
            tail = x[SSD_L - TAIL_ROWS:SSD_L, :]
        tail_ref[:, cs] = tail
        u = jnp.concatenate(convs, axis=0)
        ctx[q] = (u, u.astype(BF16))

    def strip_mm(q):
        u, ub = ctx[q]
        ctx[q] = (u, jnp.dot(ub, wa_ref[q], preferred_element_type=F32),
                  jnp.dot(ub, wx_ref[q], preferred_element_type=F32))

    def strip_post(q):
        cs = slice(q * LRU_STRIP, (q + 1) * LRU_STRIP)
        u, r_lin, i_lin = ctx.pop(q)
        r = jax.nn.sigmoid(r_lin + ba_ref[:, cs])
        i = jax.nn.sigmoid(i_lin + bx_ref[:, cs])
        log_a = -LRU_C * r * jax.nn.softplus(-lam_ref[:, cs])
        a = jnp.exp(log_a)
        a_ref[:, cs] = a
        u_ref[:, cs] = jnp.sqrt(-jnp.tanh(log_a) * (a * a + 1.0)) * (i * u)

    for q in range(width // LRU_STRIP):
        steps.append((525, functools.partial(strip_pre, q), functools.partial(strip_mm, q),
                      functools.partial(strip_post, q)))

    def scan(r0):
        sub = lax.broadcasted_iota(jnp.int32, (SUBLANES, width), 0)
        tile_rows = lambda j: slice(r0 + j * SUBLANES, r0 + (j + 1) * SUBLANES)
        h = p = None
        for j in range(CHUNK_BLOCKS):
            a = a_ref[tile_rows(j), :]
            b = u_ref[tile_rows(j), :]
            h = b if j == 0 else a * h + b
            p = a if j == 0 else a * p
            u_ref[tile_rows(j), :] = h
            a_ref[tile_rows(j), :] = p
        for k in (1, 2, 4):
            keep = sub >= k
            p_sh = jnp.where(keep, pltpu.roll(p, k, 0), 1.0)
            h_sh = jnp.where(keep, pltpu.roll(h, k, 0), 0.0)
            h = p * h_sh + h
            p = p * p_sh
        h0 = hc_ref[...]
        ends = h + p * h0
        carry = jnp.where(sub == 0, h0, pltpu.roll(ends, 1, 0))
        hc_ref[...] = jnp.broadcast_to(ends[SUBLANES - 1:SUBLANES, :], (SUBLANES, width))
        for j in range(CHUNK_BLOCKS):
            h_true = u_ref[tile_rows(j), :] + a_ref[tile_rows(j), :] * carry
            gate = ld(tile_rows(j), slice(P_LRUG, P_LRUG + width))
            out_ref[tile_rows(j), :] = (jax.nn.gelu(gate) * h_true).astype(BF16)

    for r0 in range(0, t_len, SSD_L):
        steps.append((500, None, None, functools.partial(scan, r0)))
    return steps


def _split3(v):
    hi = v.astype(BF16)
    r1 = v - hi.astype(F32)
    mid = r1.astype(BF16)
    lo = (r1 - mid.astype(F32)).astype(BF16)
    return hi, mid, lo


def _ssd_steps(ld, ld_dt_t, ld_tail, r0, cw_ref, cb_ref, dtb_ref, dtb_t_ref,
               alog_ref, alog_t_ref, dskip_ref, ng_ref, out_ref, st_ref):
    l_len = SSD_L
    rs = slice(r0, r0 + l_len)
    ctx = {}

    def conv_silu(c0, width):
        cs = slice(c0, c0 + width)
        x = ld(rs, slice(P_XBC + c0, P_XBC + c0 + width))
        return jax.nn.silu(_causal_conv(x, ld_tail(cs), cw_ref, cb_ref, cs))

    def prologue_pre():
        def time_of(axis):
            r = lax.broadcasted_iota(jnp.int32, (l_len, l_len), axis)
            return (r % SUBLANES) * CHUNK_BLOCKS + r // SUBLANES

        rows = time_of(0)
        cols = time_of(1)
        causal = rows >= cols
        dt = jax.nn.softplus(ld(rs, slice(P_DT, P_DT + DT_PAD)) + dtb_ref[...])
        dt_t = jax.nn.softplus(ld_dt_t() + dtb_t_ref[...])
        lane = lax.broadcasted_iota(jnp.int32, (l_len, LANES), 1)
        ctx.update(
            causal=causal, dt=dt, first_head=lane < SSD_HEAD_DIM,
            lower=jnp.where(causal, 1.0, 0.0).astype(BF16),
            upper=jnp.where(rows <= cols, 1.0, 0.0).astype(BF16),
            a_parts=_split3(dt * -jnp.exp(alog_ref[...])),
            at_parts=_split3(dt_t * -jnp.exp(alog_t_ref[...])),
            b_all=conv_silu(SSD_INNER, SSD_BC).astype(BF16),
            c_all=conv_silu(SSD_INNER + SSD_BC, SSD_BC).astype(BF16))

    def scores_mm(g):
        b_g = ctx["b_all"][:, g * SSD_STATE:(g + 1) * SSD_STATE]
        c_g = ctx["c_all"][:, g * SSD_STATE:(g + 1) * SSD_STATE]
        scores = lax.dot_general(c_g, b_g, (((1,), (1,)), ((), ())),
                                 preferred_element_type=F32)
        ctx[g] = dict(b_g=b_g, c_g=c_g, scores=scores)

    def prologue_mm():
        lower, upper = ctx.pop("lower"), ctx.pop("upper")
        ctx["a_cs"] = sum(jnp.dot(lower, p, preferred_element_type=F32)
                          for p in reversed(ctx.pop("a_parts")))
        ctx["a_cs_t"] = sum(jnp.dot(p, upper, preferred_element_type=F32)
                            for p in reversed(ctx.pop("at_parts")))
        scores_mm(0)

    def expand(q, head0):
        tiles = []
        for j in range(PAIRS_PER_GROUP):
            h_a = head0 + 2 * j
            tiles.append(jnp.where(ctx["first_head"], q[:, h_a:h_a + 1], q[:, h_a + 1:h_a + 2]))
        return jnp.concatenate(tiles, axis=1)

    def group_pre(g):
        causal, a_cs, a_cs_t, first_head = (
            ctx["causal"], ctx["a_cs"], ctx["a_cs_t"], ctx["first_head"])
        c = ctx[g]
        gs = slice(g * GROUP_W, (g + 1) * GROUP_W)
        head0 = g * SSD_HEADS_PER_GROUP
        x_g = conv_silu(g * GROUP_W, GROUP_W)
        acs_x = expand(a_cs, head0)
        xdt = x_g * expand(ctx["dt"], head0)
        xdt_b = xdt.astype(BF16)
        lhs, rhs = [], []
        for j in range(PAIRS_PER_GROUP):
            x_pair = xdt_b[:, j * LANES:(j + 1) * LANES]
            m_pair, x_halves = [], []
            for half in range(2):
                h = head0 + 2 * j + half
                seg = a_cs[:, h:h + 1] - a_cs_t[h:h + 1, :]
                decay = jnp.exp(jnp.where(causal, seg, -jnp.inf))
                m_pair.append((c["scores"] * decay).astype(BF16))
                keep = first_head if half == 0 else jnp.logical_not(first_head)
                x_halves.append(jnp.where(keep, x_pair, jnp.zeros_like(x_pair)))
            lhs.append(jnp.concatenate(m_pair, axis=1))
            rhs.append(jnp.concatenate(x_halves, axis=0))
        last = acs_x[l_len - 1:l_len, :]
        st = st_ref[:, gs]
        c.update(x_g=x_g, lhs=lhs, rhs=rhs, e_cs=jnp.exp(acs_x), st_b=st.astype(BF16),
                 st_decayed=st * jnp.exp(last),
                 x_end=(xdt * jnp.exp(last - acs_x)).astype(BF16))

    def group_mm(g):
        c = ctx[g]
        lhs, rhs = c.pop("lhs"), c.pop("rhs")
        c["y_diag"] = jnp.concatenate(
            [jnp.dot(a, b, preferred_element_type=F32) for a, b in zip(lhs, rhs)], axis=1)
        c["y_off"] = jnp.dot(c["c_g"], c.pop("st_b"), preferred_element_type=F32)
        c["st_new"] = lax.dot_general(c["b_g"], c.pop("x_end"), (((0,), (0,)), ((), ())),
                                      preferred_element_type=F32)
        if g + 1 < SSD_GROUPS:
            scores_mm(g + 1)

    def group_post(g):
        c = ctx.pop(g)
        gs = slice(g * GROUP_W, (g + 1) * GROUP_W)
        st_ref[:, gs] = c["st_decayed"] + c["st_new"]
        y = c["y_diag"] + c["y_off"] * c["e_cs"] + dskip_ref[:, gs] * c["x_g"]
        y = y * jax.nn.silu(ld(rs, slice(P_Z + gs.start, P_Z + gs.stop)))
        y = y * lax.rsqrt(jnp.mean(y * y, axis=-1, keepdims=True) + EPS)
        out_ref[rs, gs] = (y * ng_ref[:, gs]).astype(BF16)

    steps = [(600, prologue_pre, prologue_mm, None)]
    for g in range(SSD_GROUPS):
        steps.append((950, functools.partial(group_pre, g), functools.partial(group_mm, g),
                      functools.partial(group_post, g)))
    return steps


def _mixer_in_kernel(h_ref, g_ref, w_ref, wdt_t_ref,
                     lcw_ref, lcb_ref, wa_ref, ba_ref, wx_ref, bx_ref, lam_ref,
                     scw_ref, scb_ref, dtb_ref, dtb_t_ref, alog_ref, alog_t_ref, dskip_ref, ng_ref,
                     ya_ref, yb_ref, gates_ref,
                     p_ref, dtt_ref, ltail_ref, a_ref, u_ref, hc_ref, stail_ref, st_ref,
                     *, tiles_per_seq, permute_input):
    i = pl.program_id(0)
    slot = lax.rem(i, 2)
    prev = 1 - slot

    @pl.when(i == 0)
    def _():
        p_ref[1] = jnp.zeros(p_ref.shape[1:], F32)
        dtt_ref[1] = jnp.zeros(dtt_ref.shape[1:], F32)

    @pl.when(jnp.logical_or(i == 0, lax.rem(i + tiles_per_seq - 1, tiles_per_seq) == 0))
    def _():
        ltail_ref[...] = jnp.zeros_like(ltail_ref)
        hc_ref[...] = jnp.zeros_like(hc_ref)
        stail_ref[...] = jnp.zeros_like(stail_ref)
        st_ref[...] = jnp.zeros_like(st_ref)

    h = h_ref[...]
    if permute_input:
        h = _to_permuted_rows(h)
    xb = _rms(h, g_ref[...]).astype(BF16)

    def proj_cols(c, stop):
        p_ref[slot, :, c:stop] = jnp.dot(xb, w_ref[:, c:stop], preferred_element_type=F32)

    def proj_gates(c):
        gates_ref[:, c:c + PROJ_CHUNK] = jnp.dot(
            xb, w_ref[:, W_GATES + c:W_GATES + c + PROJ_CHUNK], preferred_element_type=F32)

    def proj_dt_t():
        dtt_ref[slot] = lax.dot_general(wdt_t_ref[...], xb, (((1,), (1,)), ((), ())),
                                        preferred_element_type=F32)

    proj_steps = [(min(c + PROJ_CHUNK, P_COLS) - c,
                   functools.partial(proj_cols, c, min(c + PROJ_CHUNK, P_COLS)))
                  for c in range(0, P_COLS, PROJ_CHUNK)]
    proj_steps += [(PROJ_CHUNK, functools.partial(proj_gates, c))
                   for c in range(0, 2 * D_MODEL, PROJ_CHUNK)]
    proj_steps.append((SSD_HEADS, proj_dt_t))

    def ld(rows, cols):
        return p_ref[prev, rows, cols]

    mixer_steps = _lru_steps(ld, lcw_ref, lcb_ref, wa_ref, ba_ref, wx_ref, bx_ref, lam_ref,
                             ya_ref, ltail_ref, a_ref, u_ref, hc_ref)
    for r0 in range(0, MIX_T, SSD_L):
        if r0 == 0:
            ld_tail = lambda cs: stail_ref[:, cs]
        else:
            ld_tail = lambda cs, r0=r0: ld(slice(r0 - TAIL_ROWS, r0),
                                           slice(P_XBC + cs.start, P_XBC + cs.stop))
        mixer_steps += _ssd_steps(
            ld, lambda r0=r0: dtt_ref[prev, :, r0:r0 + SSD_L], ld_tail, r0, scw_ref, scb_ref,
            dtb_ref, dtb_t_ref, alog_ref, alog_t_ref, dskip_ref, ng_ref, yb_ref, st_ref)

    def save_tail():
        stail_ref[...] = ld(slice(MIX_T - TAIL_ROWS, MIX_T), slice(P_XBC, P_XBC + SSD_CONV_DIM))

    mixer_steps.append((1, None, None, save_tail))
    _run_interleaved(mixer_steps, proj_steps)


def _mixer_in(h, g, w, wdt_t, lru_params, ssd_params, seq, layer, permute_input):
    m = h.shape[0]
    n = m // MIX_T
    cur = lambda width: pl.BlockSpec((MIX_T, width), lambda i: (jnp.minimum(i, n - 1), 0))
    lag = lambda width: pl.BlockSpec((MIX_T, width), lambda i: (jnp.maximum(i - 1, 0), 0))
    n_strip = LRU_WIDTH // LRU_STRIP
    lru_specs = [_const_spec((CONV_W, LRU_WIDTH)), _const_spec((1, LRU_WIDTH)),
                 _const_spec((n_strip, LRU_STRIP, LRU_STRIP)), _const_spec((1, LRU_WIDTH)),
                 _const_spec((n_strip, LRU_STRIP, LRU_STRIP)), _const_spec((1, LRU_WIDTH)),
                 _const_spec((1, LRU_WIDTH))]
    ssd_specs = [_const_spec((CONV_W, SSD_CONV_DIM)), _const_spec((1, SSD_CONV_DIM)),
                 _const_spec((1, DT_PAD)), _const_spec((SSD_HEADS, 1)),
                 _const_spec((1, DT_PAD)), _const_spec((SSD_HEADS, 1)),
                 _const_spec((1, SSD_INNER)), _const_spec((1, SSD_INNER))]
    return pl.pallas_call(
        functools.partial(_mixer_in_kernel, tiles_per_seq=seq // MIX_T,
                          permute_input=permute_input),
        grid=(n + 1,),
        in_specs=[cur(D_MODEL), _const_spec((1, D_MODEL)),
                  _layer_spec((D_MODEL, W_COLS), layer),
                  _layer_spec((SSD_HEADS, D_MODEL), layer)] + lru_specs + ssd_specs,
        out_specs=[lag(LRU_WIDTH), lag(SSD_INNER), cur(2 * D_MODEL)],
        out_shape=[jax.ShapeDtypeStruct((m, LRU_WIDTH), BF16),
                   jax.ShapeDtypeStruct((m, SSD_INNER), BF16),
                   jax.ShapeDtypeStruct((m, 2 * D_MODEL), F32)],
        scratch_shapes=[pltpu.VMEM((2, MIX_T, P_COLS), F32),
                        pltpu.VMEM((2, SSD_HEADS, MIX_T), F32),
                        pltpu.VMEM((TAIL_ROWS, LRU_WIDTH), F32),
                        pltpu.VMEM((MIX_T, LRU_WIDTH), F32),
                        pltpu.VMEM((MIX_T, LRU_WIDTH), F32),
                        pltpu.VMEM((SUBLANES, LRU_WIDTH), F32),
                        pltpu.VMEM((TAIL_ROWS, SSD_CONV_DIM), F32),
                        pltpu.VMEM((SSD_STATE, SSD_INNER), F32)],
        compiler_params=pltpu.CompilerParams(
            dimension_semantics=("arbitrary",), vmem_limit_bytes=VMEM_LIMIT_BYTES),
        name="mixer_in",
    )(h, g, w, wdt_t, *lru_params, *ssd_params)


def _merge_ffn_kernel(h_ref, ya_ref, yb_ref, gates_ref, bg_ref, wbr_ref, wout_ref,
                      g2_ref, w1_ref, w2_ref, gf_ref, out_ref, *, final_norm, permute_input):
    g = jax.nn.sigmoid(gates_ref[...] + bg_ref[...])
    p_a = jnp.dot(ya_ref[...], wbr_ref[0:LRU_WIDTH, :], preferred_element_type=F32)
    p_b = jnp.dot(yb_ref[...], wbr_ref[LRU_WIDTH:, :], preferred_element_type=F32)
    merged = g[:, :D_MODEL] * p_a + g[:, D_MODEL:] * p_b
    h = h_ref[...]
    if permute_input:
        h = _to_permuted_rows(h)
    h = h + jnp.dot(merged.astype(BF16), wout_ref[...], preferred_element_type=F32)

    xb = _rms(h, g2_ref[...]).astype(BF16)
    acc = h
    for c in range(0, D_FF, FFN_CHUNK):
        gate = jnp.dot(xb, w1_ref[:, c:c + FFN_CHUNK], preferred_element_type=F32)
        up = jnp.dot(xb, w1_ref[:, D_FF + c:D_FF + c + FFN_CHUNK], preferred_element_type=F32)
        act = (jax.nn.silu(gate) * up).astype(BF16)
        acc = acc + jnp.dot(act, w2_ref[c:c + FFN_CHUNK, :], preferred_element_type=F32)
    if final_norm:
        acc = _rms(acc, gf_ref[...])
        for r0 in range(0, acc.shape[0], SSD_L):
            chunk = acc[r0:r0 + SSD_L, :].reshape(CHUNK_BLOCKS, SUBLANES, D_MODEL)
            out_ref[r0:r0 + SSD_L, :] = pltpu.einshape("jsd->sjd", chunk).reshape(SSD_L, D_MODEL)
    else:
        out_ref[...] = acc


def _merge_ffn(h, y_a, y_b, gates, bg, wbr, wout, g2, w1, w2, gf, final_norm, layer,
               permute_input):
    m = h.shape[0]
    row = lambda width: pl.BlockSpec((FFN_TM, width), lambda i: (i, 0))
    return pl.pallas_call(
        functools.partial(_merge_ffn_kernel, final_norm=final_norm,
                          permute_input=permute_input),
        grid=(m // FFN_TM,),
        in_specs=[row(D_MODEL), row(LRU_WIDTH), row(SSD_INNER), row(2 * D_MODEL),
                  _const_spec((1, 2 * D_MODEL)),
                  _layer_spec((LRU_WIDTH + SSD_INNER, D_MODEL), layer),
                  _layer_spec((D_MODEL, D_MODEL), layer),
                  _const_spec((1, D_MODEL)), _layer_spec((D_MODEL, 2 * D_FF), layer),
                  _layer_spec((D_FF, D_MODEL), layer), _const_spec((1, D_MODEL))],
        out_specs=row(D_MODEL),
        out_shape=jax.ShapeDtypeStruct((m, D_MODEL), F32),
        compiler_params=pltpu.CompilerParams(
            dimension_semantics=("arbitrary",), vmem_limit_bytes=VMEM_LIMIT_BYTES),
        name="merge_ffn",
    )(h, y_a, y_b, gates, bg, wbr, wout, g2, w1, w2, gf)


def _block_diag_tiles(w):
    per = LRU_STRIP // LRU_BLOCK_DIM
    w4 = w.reshape(LRU_BLOCKS // per, per, LRU_BLOCK_DIM, LRU_BLOCK_DIM)
    eye = jnp.eye(per, dtype=w.dtype)
    return jnp.einsum("qaij,ab->qaibj", w4, eye).reshape(
        LRU_BLOCKS // per, LRU_STRIP, LRU_STRIP).astype(BF16)


def _pad_lanes(v, width):
    return jnp.pad(v, ((0, 0), (0, width - v.shape[-1])))


def kernel(x, norm1_g, w_in, b_gate, lru_conv_w, lru_conv_b, lru_w_a, lru_b_a, lru_w_x, lru_b_x,
           lru_lambda, ssd_conv_w, ssd_conv_b, ssd_dt_bias, ssd_A_log, ssd_D, ssd_norm_g,
           w_branch, w_out, norm2_g, w_ffn_in, w_ffn_out, norm_f):
    batch, seq, d = x.shape
    depth = w_in.shape[0]
    m = batch * seq
    assert d == D_MODEL and seq % MIX_T == 0 and MIX_T % SSD_L == 0 and m % FFN_TM == 0

    o_dt = 2 * LRU_WIDTH + SSD_INNER + SSD_CONV_DIM
    o_g = o_dt + SSD_HEADS
    row2 = lambda v: v.reshape(1, -1)

    w_dt = w_in[:, :, o_dt:o_g]
    w_in_b = jnp.concatenate(
        [w_in[:, :, :o_dt].astype(BF16),
         jnp.pad(w_dt.astype(BF16), ((0, 0), (0, 0), (0, DT_PAD - SSD_HEADS))),
         w_in[:, :, o_g:].astype(BF16)], axis=2)
    w_dt_t_b = jnp.swapaxes(w_dt, 1, 2).astype(BF16)
    w_branch_b, w_out_b = w_branch.astype(BF16), w_out.astype(BF16)
    w_ffn_in_b, w_ffn_out_b = w_ffn_in.astype(BF16), w_ffn_out.astype(BF16)

    h = x.reshape(m, d)
    for l in range(depth):
        lru_params = (lru_conv_w[l], row2(lru_conv_b[l]),
                      _block_diag_tiles(lru_w_a[l]), row2(lru_b_a[l]),
                      _block_diag_tiles(lru_w_x[l]), row2(lru_b_x[l]), row2(lru_lambda[l]))
        ssd_params = (ssd_conv_w[l], row2(ssd_conv_b[l]),
                      _pad_lanes(row2(ssd_dt_bias[l]), DT_PAD), ssd_dt_bias[l].reshape(-1, 1),
                      _pad_lanes(row2(ssd_A_log[l]), DT_PAD), ssd_A_log[l].reshape(-1, 1),
                      row2(jnp.repeat(ssd_D[l], SSD_HEAD_DIM)), row2(ssd_norm_g[l]))
        y_a, y_b, gates = _mixer_in(h, row2(norm1_g[l]), w_in_b, w_dt_t_b,
                                    lru_params, ssd_params, seq, l, permute_input=(l == 0))
        h = _merge_ffn(h, y_a, y_b, gates, row2(b_gate[l]), w_branch_b, w_out_b,
                       row2(norm2_g[l]), w_ffn_in_b, w_ffn_out_b,
                       row2(norm_f), final_norm=(l == depth - 1), layer=l,
                       permute_input=(l == 0))
    return h.reshape(batch, seq, d)
```
